```python
import math
import jax, jax.numpy as jnp
from jax import lax
import numpy as np

D_MODEL = 1024
BATCH = 2
SEQ = 8192
DEPTH = 2
DEC_BATCH = 32
DEC_SEQ = 4
PAST_LEN = 8192
PAGE_SIZE = 128

N_MIXERS = 2
N_HEADS = 8
HEAD_DIM = 64
QK_DIM = 2 * HEAD_DIM
V_DIM = 2 * HEAD_DIM
ATTN_WIDTH = N_HEADS * V_DIM
N_BUCKETS = 32
MAX_DISTANCE = 128
CONV_WIDTH = 3
D_FF = -(-8 * D_MODEL // (3 * 256)) * 256
BLOCK_Q = 128
EPS = 1e-6
NEG_INF = -1e30

kernel_name = "hybrid_diffattn_shortconv_decode_step"


def lambda_init(layer):
    return 0.8 - 0.6 * math.exp(-0.3 * layer)


def rms_norm(x, g):
    xf = x.astype(jnp.float32)
    y = xf * lax.rsqrt(jnp.mean(xf * xf, axis=-1, keepdims=True) + EPS) * g.astype(jnp.float32)
    return y.astype(x.dtype)


def rel_position_bias(q_pos, k_pos, table):
    n = jnp.maximum(q_pos[:, None] - k_pos[None, :], 0)
    max_exact = N_BUCKETS // 2
    nf = jnp.maximum(n, 1).astype(jnp.float32)
    large = max_exact + (jnp.log(nf / max_exact) / math.log(MAX_DISTANCE / max_exact)
                         * (N_BUCKETS - max_exact)).astype(jnp.int32)
    large = jnp.minimum(large, N_BUCKETS - 1)
    bucket = jnp.where(n < max_exact, n, large)
    return jnp.transpose(table[bucket].astype(jnp.float32), (2, 0, 1))


def diff_lambda(lq1, lk1, lq2, lk2, lam_init):
    f32 = jnp.float32
    return (jnp.exp(jnp.sum(lq1.astype(f32) * lk1.astype(f32)))
            - jnp.exp(jnp.sum(lq2.astype(f32) * lk2.astype(f32))) + lam_init)


def qkv_project(h, w_qkv):
    b, t, _ = h.shape
    q, k, v = jnp.split(h @ w_qkv, [N_HEADS * QK_DIM, 2 * N_HEADS * QK_DIM], axis=-1)
    return (q.reshape(b, t, N_HEADS, 2, HEAD_DIM),
            k.reshape(b, t, N_HEADS, QK_DIM),
            v.reshape(b, t, N_HEADS, V_DIM))


def diff_logits(q, k):
    b, s = k.shape[0], k.shape[1]
    k5 = k.reshape(b, s, N_HEADS, 2, HEAD_DIM)
    return jnp.einsum('bqhcd,bkhcd->cbhqk', q, k5,
                      preferred_element_type=jnp.float32) * (HEAD_DIM ** -0.5)


def diff_weights(scores, lam):
    p = jax.nn.softmax(scores, axis=-1)
    return p[0] - lam * p[1]


def diff_head_out(o, subln_g, lam_init, w_o, dtype):
    b, t = o.shape[0], o.shape[1]
    on = o * lax.rsqrt(jnp.mean(o * o, axis=-1, keepdims=True) + EPS) * subln_g.astype(jnp.float32)
    on = (on * (1.0 - lam_init)).reshape(b, t, ATTN_WIDTH).astype(dtype)
    return on @ w_o


def diff_attn_prompt(h, w_qkv, w_o, lq1, lk1, lq2, lk2, subln_g, rel_bias, lam_init):
    b, s, _ = h.shape
    q, k, v = qkv_project(h, w_qkv)
    lam = diff_lambda(lq1, lk1, lq2, lk2, lam_init)
    k_pos = jnp.arange(s)

    def block(start):
        qb = lax.dynamic_slice_in_dim(q, start, BLOCK_Q, axis=1)
        q_pos = start + jnp.arange(BLOCK_Q)
        sc = diff_logits(qb, k) + rel_position_bias(q_pos, k_pos, rel_bias)
        sc = jnp.where(k_pos[None, :] <= q_pos[:, None], sc, NEG_INF)
        w = diff_weights(sc, lam)
        return jnp.einsum('bhqk,bkhv->bqhv', w.astype(v.dtype), v,
                          preferred_element_type=jnp.float32)

    starts = jnp.arange(s // BLOCK_Q) * BLOCK_Q
    o = lax.map(block, starts)
    o = jnp.moveaxis(o, 0, 1).reshape(b, s, N_HEADS, V_DIM)
    return diff_head_out(o, subln_g, lam_init, w_o, h.dtype), k, v


def diff_attn_sample(h, cache_k, cache_v, page_table, w_qkv, w_o, lq1, lk1, lq2, lk2,
                     subln_g, rel_bias, lam_init):
    b, t, _ = h.shape
    q, k, v = qkv_project(h, w_qkv)
    lam = diff_lambda(lq1, lk1, lq2, lk2, lam_init)
    past = page_table.shape[1] * PAGE_SIZE
    past_k = cache_k[page_table].reshape(b, past, N_HEADS, QK_DIM)
    past_v = cache_v[page_table].reshape(b, past, N_HEADS, V_DIM)
    q_pos = past + jnp.arange(t)
    k_pos = jnp.arange(past + t)
    sc = jnp.concatenate([diff_logits(q, past_k), diff_logits(q, k)], axis=-1)
    sc = sc + rel_position_bias(q_pos, k_pos, rel_bias)
    sc = jnp.where(k_pos[None, :] <= q_pos[:, None], sc, NEG_INF)
    w = diff_weights(sc, lam)
    o = (jnp.einsum('bhqk,bkhv->bqhv', w[..., :past].astype(past_v.dtype), past_v,
                    preferred_element_type=jnp.float32)
         + jnp.einsum('bhqk,bkhv->bqhv', w[..., past:].astype(v.dtype), v,
                      preferred_element_type=jnp.float32))
    return diff_head_out(o, subln_g, lam_init, w_o, h.dtype), k, v


def short_conv(h, conv_state, w_in, conv_w, w_out):
    t = h.shape[1]
    gb, gc, xt = jnp.split(h @ w_in, 3, axis=-1)
    u = gc * xt
    u_ext = jnp.concatenate([conv_state.astype(u.dtype), u], axis=1)
    y = sum(conv_w[j] * u_ext[:, j:j + t] for j in range(CONV_WIDTH))
    return (gb * y) @ w_out, u_ext[:, -(CONV_WIDTH - 1):]


def swiglu(h, w_gu, w_down):
    g, u = jnp.split(h @ w_gu, 2, axis=-1)
    return (jax.nn.silu(g) * u) @ w_down


def setup_inputs(seed: int = 0) -> dict:
    key = jax.random.key(seed)
    ks = jax.random.split(key, 24)
    f32 = jnp.float32
    nrm = lambda k, shape, scale: jax.random.normal(k, shape, f32) * scale
    n_pages = PAST_LEN // PAGE_SIZE
    n_used = DEC_BATCH * n_pages
    n_pool = n_used + max(1, n_used // 4)
    page_table = jax.random.permutation(ks[0], n_pool)[:n_used].reshape(DEC_BATCH, n_pages).astype(jnp.int32)
    return {
        "x_prompt": nrm(ks[1], (BATCH, SEQ, D_MODEL), 1.0),
        "x_sample": nrm(ks[2], (DEC_BATCH, DEC_SEQ, D_MODEL), 1.0),
        "cache_k": nrm(ks[3], (n_pool, PAGE_SIZE, N_HEADS, QK_DIM), 1.0),
        "cache_v": nrm(ks[4], (n_pool, PAGE_SIZE, N_HEADS, V_DIM), 1.0),
        "state_conv": nrm(ks[5], (DEC_BATCH, CONV_WIDTH - 1, D_MODEL), 1.0),
        "page_table": page_table,
        "rel_bias": nrm(ks[6], (N_BUCKETS, N_HEADS), 0.5),
        "w_qkv": nrm(ks[7], (D_MODEL, 2 * N_HEADS * QK_DIM + ATTN_WIDTH), D_MODEL ** -0.5),
        "w_o": nrm(ks[8], (ATTN_WIDTH, D_MODEL), ATTN_WIDTH ** -0.5),
        "lambda_q1": nrm(ks[9], (HEAD_DIM,), 0.1),
        "lambda_k1": nrm(ks[10], (HEAD_DIM,), 0.1),
        "lambda_q2": nrm(ks[11], (HEAD_DIM,), 0.1),
        "lambda_k2": nrm(ks[12], (HEAD_DIM,), 0.1),
        "subln_g": 1.0 + nrm(ks[13], (V_DIM,), 0.02),
        "w_conv_in": nrm(ks[14], (D_MODEL, 3 * D_MODEL), D_MODEL ** -0.5),
        "conv_w": nrm(ks[15], (CONV_WIDTH, D_MODEL), CONV_WIDTH ** -0.5),
        "w_conv_out": nrm(ks[16], (D_MODEL, D_MODEL), D_MODEL ** -0.5),
        "norm_mix_pre": 1.0 + nrm(ks[17], (DEPTH, D_MODEL), 0.02),
        "norm_mix_post": 1.0 + nrm(ks[18], (DEPTH, D_MODEL), 0.02),
        "norm_ffn_pre": 1.0 + nrm(ks[19], (DEPTH, D_MODEL), 0.02),
        "norm_ffn_post": 1.0 + nrm(ks[20], (DEPTH, D_MODEL), 0.02),
        "w_ffn_gu": nrm(ks[21], (DEPTH, D_MODEL, 2 * D_FF), D_MODEL ** -0.5),
        "w_ffn_down": nrm(ks[22], (DEPTH, D_FF, D_MODEL), D_FF ** -0.5),
    }


def reference(x_prompt, x_sample, cache_k, cache_v, state_conv, page_table, rel_bias,
              w_qkv, w_o, lambda_q1, lambda_k1, lambda_q2, lambda_k2, subln_g,
              w_conv_in, conv_w, w_conv_out,
              norm_mix_pre, norm_mix_post, norm_ffn_pre, norm_ffn_post,
              w_ffn_gu, w_ffn_down):
    xp, xs = x_prompt, x_sample
    for i in range(DEPTH):
        hp = rms_norm(xp, norm_mix_pre[i])
        hs = rms_norm(xs, norm_mix_pre[i])
        if i % N_MIXERS == 0:
            lam0 = lambda_init(i)
            mp, k_p, v_p = diff_attn_prompt(hp, w_qkv, w_o, lambda_q1, lambda_k1, lambda_q2,
                                            lambda_k2, subln_g, rel_bias, lam0)
            ms, k_s, v_s = diff_attn_sample(hs, cache_k, cache_v, page_table, w_qkv, w_o,
                                            lambda_q1, lambda_k1, lambda_q2, lambda_k2,
                                            subln_g, rel_bias, lam0)
        else:
            zero_state = jnp.zeros((hp.shape[0], CONV_WIDTH - 1, D_MODEL), hp.dtype)
            mp, conv_p = short_conv(hp, zero_state, w_conv_in, conv_w, w_conv_out)
            ms, conv_s = short_conv(hs, state_conv, w_conv_in, conv_w, w_conv_out)
        xp = xp + rms_norm(mp, norm_mix_post[i])
        xs = xs + rms_norm(ms, norm_mix_post[i])
        xp = xp + rms_norm(swiglu(rms_norm(xp, norm_ffn_pre[i]), w_ffn_gu[i], w_ffn_down[i]), norm_ffn_post[i])
        xs = xs + rms_norm(swiglu(rms_norm(xs, norm_ffn_pre[i]), w_ffn_gu[i], w_ffn_down[i]), norm_ffn_post[i])
    return (xp, xs, k_p, v_p, k_s, v_s, conv_p, conv_s)
```

```python
import functools
import math

import jax
import jax.numpy as jnp
from jax import lax
from jax.experimental import pallas as pl
from jax.experimental.pallas import tpu as pltpu

D_MODEL = 1024
N_HEADS = 8
HEAD_DIM = 64
QK_DIM = 2 * HEAD_DIM
V_DIM = 2 * HEAD_DIM
ATTN_WIDTH = N_HEADS * V_DIM
N_BUCKETS = 32
MAX_DISTANCE = 128
CONV_WIDTH = 3
PAGE_SIZE = 128
EPS = 1e-6
NEG_INF = -1e30
SCALE = HEAD_DIM ** -0.5

V7X_LANES = 128
V7X_SUBLANES = 8
VMEM_LIMIT_BYTES = 56 * 1024 * 1024

F32 = jnp.float32
BF16 = jnp.bfloat16


def _lambda_init(layer):
    return 0.8 - 0.6 * math.exp(-0.3 * layer)


def _bucket_uppers():
    max_exact = N_BUCKETS // 2
    n_far = None
    buckets = []
    for n in range(0, 4 * MAX_DISTANCE):
        if n < max_exact:
            b = n
        else:
            r = math.log(n / max_exact) / math.log(MAX_DISTANCE / max_exact) * (N_BUCKETS - max_exact)
            frac = r - math.floor(r)
            assert min(frac, 1 - frac) > 1e-4 or n == max_exact or n >= MAX_DISTANCE, n
            b = min(max_exact + int(r + (1e-6 if n == max_exact else 0.0)), N_BUCKETS - 1)
            if n >= MAX_DISTANCE:
                b = N_BUCKETS - 1
        buckets.append(b)
    uppers = []
    for b in range(N_BUCKETS - 1):
        uppers.append(next(n for n, bb in enumerate(buckets) if bb > b))
    n_far = uppers[-1]
    return tuple(uppers), n_far


BUCKET_UPPERS, FAR_DISTANCE = _bucket_uppers()


def _shifted_bias(n, relb_ref, head):
    far = relb_ref[N_BUCKETS - 1, head]
    val = jnp.zeros(n.shape, F32)
    for b in range(N_BUCKETS - 2, -1, -1):
        val = jnp.where(n < BUCKET_UPPERS[b], relb_ref[b, head] - far, val)
    return val


def _diff_lambda(lam_ref, lam_init):
    lv = lam_ref[...]
    a = jnp.sum(lv[0:1] * lv[1:2], axis=1, keepdims=True)
    b = jnp.sum(lv[2:3] * lv[3:4], axis=1, keepdims=True)
    return jnp.exp(a) - jnp.exp(b) + lam_init


def _rms(x, g):
    return x * lax.rsqrt(jnp.mean(x * x, axis=-1, keepdims=True) + EPS) * g


def _head_norm(o, g, lam_init):
    return o * lax.rsqrt(jnp.mean(o * o, axis=-1, keepdims=True) + EPS) * g * (1.0 - lam_init)


def _const_spec(shape):
    return pl.BlockSpec(shape, lambda *_: (0,) * len(shape))


def _params(n_grid):
    return pltpu.CompilerParams(dimension_semantics=("arbitrary",) * n_grid,
                                vmem_limit_bytes=VMEM_LIMIT_BYTES)


def _qkv_kernel(x_ref, g_ref, w_ref, kf_ref, vf_ref, qb_ref, kb_ref, vb_ref):
    hb = _rms(x_ref[...], g_ref[...]).astype(BF16)
    w = ATTN_WIDTH
    q = jnp.dot(hb, w_ref[:, 0:w], preferred_element_type=F32)
    qb_ref[...] = (q * SCALE).astype(BF16)
    k = jnp.dot(hb, w_ref[:, w:2 * w], preferred_element_type=F32)
    kf_ref[...] = k
    kb_ref[...] = k.astype(BF16)
    v = jnp.dot(hb, w_ref[:, 2 * w:3 * w], preferred_element_type=F32)
    vf_ref[...] = v
    vb_ref[...] = v.astype(BF16)


def _qkv_call(x, g, w_bf16, tm):
    m = x.shape[0]
    row = lambda width: pl.BlockSpec((tm, width), lambda i: (i, 0))
    return pl.pallas_call(
        _qkv_kernel,
        grid=(m // tm,),
        in_specs=[row(D_MODEL), _const_spec((1, D_MODEL)), _const_spec(w_bf16.shape)],
        out_specs=[row(ATTN_WIDTH)] * 5,
        out_shape=[jax.ShapeDtypeStruct((m, ATTN_WIDTH), F32)] * 2
        + [jax.ShapeDtypeStruct((m, ATTN_WIDTH), BF16)] * 3,
        compiler_params=_params(1),
        name="qkv_proj",
    )(x, g, w_bf16)


def _prompt_attn_kernel(relb_ref, lam_ref, g_ref, q_ref, k_ref, v_ref, o_ref,
                        q2_ref, m_ref, l_ref, acc_ref, bias_ref, *, t, lam_init):
    head = pl.program_id(1)
    qi = pl.program_id(2)

    @pl.when(qi == 0)
    def _build_bias():
        i = lax.broadcasted_iota(jnp.int32, (t, t), 0)
        j = lax.broadcasted_iota(jnp.int32, (t, t), 1)
        bias_ref[0] = jnp.where(j <= i, _shifted_bias(i - j, relb_ref, head), NEG_INF)
        bias_ref[1] = _shifted_bias(i - j + t, relb_ref, head)

    q = q_ref[...]
    lane = lax.broadcasted_iota(jnp.int32, (t, QK_DIM), 1)
    zero = jnp.zeros_like(q)
    q2_ref[0:t] = jnp.where(lane < HEAD_DIM, q, zero)
    q2_ref[t:2 * t] = jnp.where(lane >= HEAD_DIM, q, zero)
    m_ref[...] = jnp.full(m_ref.shape, NEG_INF, F32)
    l_ref[...] = jnp.zeros(l_ref.shape, F32)
    acc_ref[...] = jnp.zeros(acc_ref.shape, F32)

    def step(kj, bias):
        start = pl.multiple_of(kj * t, t)
        kt = k_ref[pl.ds(start, t), :]
        vt = v_ref[pl.ds(start, t), :]
        s = lax.dot_general(q2_ref[...], kt, (((1,), (1,)), ((), ())),
                            preferred_element_type=F32)
        if bias is not None:
            s = s + jnp.concatenate([bias, bias], axis=0)
        m_prev = m_ref[...]
        m_new = jnp.maximum(m_prev, jnp.max(s, axis=1, keepdims=True))
        alpha = jnp.exp(m_prev - m_new)
        p = jnp.exp(s - m_new[:, 0:1])
        l_ref[...] = alpha * l_ref[...] + jnp.sum(p, axis=1, keepdims=True)
        acc_ref[...] = alpha * acc_ref[...] + jnp.dot(p.astype(BF16), vt,
                                                      preferred_element_type=F32)
        m_ref[...] = m_new

    def far_step(kj, carry):
        step(kj, None)
        return carry

    lax.fori_loop(0, qi - 1, far_step, 0)

    @pl.when(qi >= 1)
    def _near():
        step(qi - 1, bias_ref[1])

    step(qi, bias_ref[0])

    inv_l = 1.0 / l_ref[...]
    o1 = acc_ref[0:t] * inv_l[0:t]
    o2 = acc_ref[t:2 * t] * inv_l[t:2 * t]
    o = o1 - _diff_lambda(lam_ref, lam_init) * o2
    o_ref[...] = _head_norm(o, g_ref[...], lam_init).astype(BF16)


def _prompt_attn_call(qb, kb, vb, rel_bias, lam_vecs, subln_g, lam_init, t):
    b, s, _ = qb.shape
    assert s % t == 0 and t >= FAR_DISTANCE
    tile = pl.BlockSpec((None, t, QK_DIM), lambda bi, h, qi: (bi, qi, h))
    seq = pl.BlockSpec((None, s, QK_DIM), lambda bi, h, qi: (bi, 0, h))
    return pl.pallas_call(
        functools.partial(_prompt_attn_kernel, t=t, lam_init=lam_init),
        grid=(b, N_HEADS, s // t),
        in_specs=[pl.BlockSpec(memory_space=pltpu.SMEM), _const_spec((4, HEAD_DIM)),
                  _const_spec((1, V_DIM)), tile, seq, seq],
        out_specs=tile,
        out_shape=jax.ShapeDtypeStruct((b, s, ATTN_WIDTH), BF16),
        scratch_shapes=[pltpu.VMEM((2 * t, QK_DIM), BF16),
                        pltpu.VMEM((2 * t, V7X_LANES), F32),
                        pltpu.VMEM((2 * t, V7X_LANES), F32),
                        pltpu.VMEM((2 * t, V_DIM), F32),
                        pltpu.VMEM((2, t, t), F32)],
        compiler_params=_params(3),
        name="prompt_attn",
    )(rel_bias, lam_vecs, subln_g, qb, kb, vb)


def _sample_attn_kernel(pt_ref, relb_ref, lam_ref, g_ref, q_ref, kn_ref, vn_ref, kc_ref, vc_ref,
                        o_ref, m_ref, l_ref, acc_ref, *, dec_seq, n_pages, lam_init):
    del pt_ref
    page = pl.program_id(1)
    rows = N_HEADS * 2 * dec_seq
    per_head = 2 * dec_seq
    assert per_head == V7X_SUBLANES
    flat = PAGE_SIZE * N_HEADS

    @pl.when(page == 0)
    def _init():
        m_ref[...] = jnp.full(m_ref.shape, NEG_INF, F32)
        l_ref[...] = jnp.zeros(l_ref.shape, F32)
        acc_ref[...] = jnp.zeros(acc_ref.shape, F32)

    q = q_ref[...]
    row = lax.broadcasted_iota(jnp.int32, (rows, QK_DIM), 0)
    lane = lax.broadcasted_iota(jnp.int32, (rows, QK_DIM), 1)
    first_comp = (row // dec_seq) % 2 == 0
    qm = jnp.where((lane < HEAD_DIM) == first_comp, q, jnp.zeros_like(q))

    def scores(keys, width):
        s = lax.dot_general(qm, keys, (((1,), (1,)), ((), ())), preferred_element_type=F32)
        r = lax.broadcasted_iota(jnp.int32, (rows, width), 0)
        c = lax.broadcasted_iota(jnp.int32, (rows, width), 1)
        same_head = (c % N_HEADS) == (r // per_head)
        return s, r % dec_seq, c // N_HEADS, same_head

    def head_bias(dist):
        parts = [_shifted_bias(dist[h * per_head:(h + 1) * per_head], relb_ref, h)
                 for h in range(N_HEADS)]
        return jnp.concatenate(parts, axis=0)

    def update(s, vals):
        m_prev = m_ref[...]
        m_new = jnp.maximum(m_prev, jnp.max(s, axis=1, keepdims=True))
        alpha = jnp.exp(m_prev - m_new)
        p = jnp.exp(s - m_new[:, 0:1])
        l_ref[...] = alpha * l_ref[...] + jnp.sum(p, axis=1, keepdims=True)
        acc_ref[...] = alpha * acc_ref[...] + jnp.dot(p.astype(BF16), vals,
                                                      preferred_element_type=F32)
        m_ref[...] = m_new

    keys = kc_ref[...].reshape(flat, QK_DIM).astype(BF16)
    vals = vc_ref[...].reshape(flat, V_DIM).astype(BF16)
    s, tq, tk, ok = scores(keys, flat)

    @pl.when(page < n_pages - 1)
    def _far_page():
        update(jnp.where(ok, s, NEG_INF), vals)

    @pl.when(page == n_pages - 1)
    def _last_page():
        update(jnp.where(ok, s + head_bias(PAGE_SIZE + tq - tk), NEG_INF), vals)
        sn, tqn, tkn, okn = scores(kn_ref[...], kn_ref.shape[0])
        okn = okn & (tkn <= tqn)
        update(jnp.where(okn, sn + head_bias(tqn - tkn), NEG_INF), vn_ref[...])
        o = acc_ref[...] / l_ref[...]
        o2 = pltpu.roll(o, rows - dec_seq, axis=0)
        od = o - _diff_lambda(lam_ref, lam_init) * o2
        o_ref[...] = _head_norm(od, g_ref[...], lam_init)


def _sample_attn_call(q_rows, k_new, v_new, cache_k, cache_v, page_table, rel_bias, lam_vecs,
                      subln_g, lam_init, dec_seq):
    nb, rows, _ = q_rows.shape
    n_pages = page_table.shape[1]
    assert PAGE_SIZE >= FAR_DISTANCE + dec_seq
    per_b = lambda shape: pl.BlockSpec((None,) + shape, lambda b, p, pt: (b, 0, 0))
    paged = pl.BlockSpec((None, PAGE_SIZE, N_HEADS, QK_DIM), lambda b, p, pt: (pt[b, p], 0, 0, 0))
    const = lambda shape: pl.BlockSpec(shape, lambda b, p, pt: (0,) * len(shape))
    grid_spec = pltpu.PrefetchScalarGridSpec(
        num_scalar_prefetch=1,
        grid=(nb, n_pages),
        in_specs=[pl.BlockSpec(memory_space=pltpu.SMEM), const((4, HEAD_DIM)), const((1, V_DIM)),
                  per_b((rows, QK_DIM)), per_b(k_new.shape[1:]), per_b(v_new.shape[1:]),
                  paged, paged],
        out_specs=per_b((rows, V_DIM)),
        scratch_shapes=[pltpu.VMEM((rows, V7X_LANES), F32),
                        pltpu.VMEM((rows, V7X_LANES), F32),
                        pltpu.VMEM((rows, V_DIM), F32)],
    )
    return pl.pallas_call(
        functools.partial(_sample_attn_kernel, dec_seq=dec_seq, n_pages=n_pages, lam_init=lam_init),
        grid_spec=grid_spec,
        out_shape=jax.ShapeDtypeStruct((nb, rows, V_DIM), F32),
        compiler_params=_params(2),
        name="sample_attn",
    )(page_table, rel_bias, lam_vecs, subln_g, q_rows, k_new, v_new, cache_k, cache_v)


def _proj_kernel(x_ref, a_ref, w_ref, g_ref, o_ref):
    y = jnp.dot(a_ref[...], w_ref[...], preferred_element_type=F32)
    o_ref[...] = x_ref[...] + _rms(y, g_ref[...])


def _proj_call(x, a_bf16, w_bf16, g_post, tm):
    m = x.shape[0]
    row = lambda width: pl.BlockSpec((tm, width), lambda i: (i, 0))
    return pl.pallas_call(
        _proj_kernel,
        grid=(m // tm,),
        in_specs=[row(D_MODEL), row(a_bf16.shape[1]), _const_spec(w_bf16.shape),
                  _const_spec((1, D_MODEL))],
        out_specs=row(D_MODEL),
        out_shape=jax.ShapeDtypeStruct((m, D_MODEL), F32),
        compiler_params=_params(1),
        name="out_proj",
    )(x, a_bf16, w_bf16, g_post)


def _ffn_kernel(x_ref, gpre_ref, wgu_ref, wd_ref, gpost_ref, o_ref, *, d_ff):
    x = x_ref[...]
    hb = _rms(x, gpre_ref[...]).astype(BF16)
    g = jnp.dot(hb, wgu_ref[:, 0:d_ff], preferred_element_type=F32)
    u = jnp.dot(hb, wgu_ref[:, d_ff:2 * d_ff], preferred_element_type=F32)
    a = (g * (1.0 / (1.0 + jnp.exp(-g))) * u).astype(BF16)
    y = jnp.dot(a, wd_ref[...], preferred_element_type=F32)
    o_ref[...] = x + _rms(y, gpost_ref[...])


def _ffn_call(x, g_pre, wgu_bf16, wd_bf16, g_post, tm):
    m = x.shape[0]
    d_ff = wd_bf16.shape[0]
    row = pl.BlockSpec((tm, D_MODEL), lambda i: (i, 0))
    return pl.pallas_call(
        functools.partial(_ffn_kernel, d_ff=d_ff),
        grid=(m // tm,),
        in_specs=[row, _const_spec((1, D_MODEL)), _const_spec(wgu_bf16.shape),
                  _const_spec(wd_bf16.shape), _const_spec((1, D_MODEL))],
        out_specs=row,
        out_shape=jax.ShapeDtypeStruct((m, D_MODEL), F32),
        compiler_params=_params(1),
        name="swiglu_ffn",
    )(x, g_pre, wgu_bf16, wd_bf16, g_post)


def _conv_kernel(*refs, tm, seq):
    carry_mode = seq >= tm
    if carry_mode:
        (x_ref, gpre_ref, win_ref, cw_ref, wout_ref, gpost_ref, st_ref,
         o_ref, tail_ref, carry_ref) = refs
    else:
        (x_ref, gpre_ref, win_ref, cw_ref, wout_ref, gpost_ref, inj1_ref, inj2_ref,
         o_ref, u_ref) = refs
    d = D_MODEL
    x = x_ref[...]
    hb = _rms(x, gpre_ref[...]).astype(BF16)
    gc = jnp.dot(hb, win_ref[:, d:2 * d], preferred_element_type=F32)
    xt = jnp.dot(hb, win_ref[:, 2 * d:3 * d], preferred_element_type=F32)
    u = gc * xt
    row = lax.broadcasted_iota(jnp.int32, (tm, d), 0)
    r1 = pltpu.roll(u, 1, axis=0)
    r2 = pltpu.roll(u, 2, axis=0)
    if carry_mode:
        i = pl.program_id(0)
        tiles_per_seq = seq // tm

        @pl.when(i % tiles_per_seq == 0)
        def _seq_start():
            carry_ref[...] = st_ref[...]

        c = carry_ref[...]
        c1 = c[V7X_SUBLANES - 1:V7X_SUBLANES]
        c2 = c[V7X_SUBLANES - 2:V7X_SUBLANES - 1]
        prev1 = jnp.where(row == 0, c1, r1)
        prev2 = jnp.where(row == 0, c2, jnp.where(row == 1, c1, r2))
        tail = u[tm - V7X_SUBLANES:tm]
        carry_ref[...] = tail
        tail_ref[...] = tail
    else:
        t = row % seq
        prev1 = jnp.where(t >= 1, r1, inj1_ref[...])
        prev2 = jnp.where(t >= 2, r2, inj2_ref[...])
        u_ref[...] = u
    cw = cw_ref[...]
    y = cw[0:1] * prev2 + cw[1:2] * prev1 + cw[2:3] * u
    gb = jnp.dot(hb, win_ref[:, 0:d], preferred_element_type=F32)
    z = jnp.dot((gb * y).astype(BF16), wout_ref[...], preferred_element_type=F32)
    o_ref[...] = x + _rms(z, gpost_ref[...])


def _conv_call(x, g_pre, win_bf16, conv_w, wout_bf16, g_post, state, tm, seq):
    m = x.shape[0]
    nb = m // seq
    row = pl.BlockSpec((tm, D_MODEL), lambda i: (i, 0))
    common = [row, _const_spec((1, D_MODEL)), _const_spec(win_bf16.shape),
              _const_spec((CONV_WIDTH, D_MODEL)), _const_spec(wout_bf16.shape),
              _const_spec((1, D_MODEL))]
    kern = functools.partial(_conv_kernel, tm=tm, seq=seq)
    if seq >= tm:
        assert seq % tm == 0
        n_tiles = m // tm
        tiles_per_seq = seq // tm
        pad = jnp.zeros((nb, V7X_SUBLANES - (CONV_WIDTH - 1), D_MODEL), F32)
        st = jnp.concatenate([pad, state], axis=1)
        st_spec = pl.BlockSpec((None, V7X_SUBLANES, D_MODEL), lambda i: (i // tiles_per_seq, 0, 0))
        tail_spec = pl.BlockSpec((None, V7X_SUBLANES, D_MODEL), lambda i: (i, 0, 0))
        out, tails = pl.pallas_call(
            kern, grid=(n_tiles,),
            in_specs=common + [st_spec],
            out_specs=[row, tail_spec],
            out_shape=[jax.ShapeDtypeStruct((m, D_MODEL), F32),
                       jax.ShapeDtypeStruct((n_tiles, V7X_SUBLANES, D_MODEL), F32)],
            scratch_shapes=[pltpu.VMEM((V7X_SUBLANES, D_MODEL), F32)],
            compiler_params=_params(1),
            name="short_conv_seq",
        )(x, g_pre, win_bf16, conv_w, wout_bf16, g_post, st)
        last = tails.reshape(nb, tiles_per_seq, V7X_SUBLANES, D_MODEL)[:, -1]
        return out, last[:, V7X_SUBLANES - (CONV_WIDTH - 1):]
    assert tm % seq == 0 and m == tm and seq >= CONV_WIDTH - 1
    zeros = lambda n: jnp.zeros((nb, n, D_MODEL), F32)
    inj1 = jnp.concatenate([state[:, 1:2], zeros(seq - 1)], axis=1).reshape(m, D_MODEL)
    inj2 = jnp.concatenate([state[:, 0:2], zeros(seq - 2)], axis=1).reshape(m, D_MODEL)
    out, u = pl.pallas_call(
        kern, grid=(1,),
        in_specs=common + [row, row],
        out_specs=[row, row],
        out_shape=[jax.ShapeDtypeStruct((m, D_MODEL), F32)] * 2,
        compiler_params=_params(1),
        name="short_conv_tok",
    )(x, g_pre, win_bf16, conv_w, wout_bf16, g_post, inj1, inj2)
    return out, u.reshape(nb, seq, D_MODEL)[:, seq - (CONV_WIDTH - 1):]


def kernel(x_prompt, x_sample, cache_k, cache_v, state_conv, page_table, rel_bias, w_qkv, w_o, lambda_q1, lambda_k1, lambda_q2, lambda_k2, subln_g, w_conv_in, conv_w, w_conv_out, norm_mix_pre, norm_mix_post, norm_ffn_pre, norm_ffn_post, w_ffn_gu, w_ffn_down):
    b, s, d = x_prompt.shape
    nb, dec_seq, _ = x_sample.shape
    assert d == D_MODEL and page_table.shape[1] * PAGE_SIZE >= FAR_DISTANCE
    mp, ms = b * s, nb * dec_seq
    tm_p, t_attn = 256, 256

    wqkv = w_qkv.astype(BF16)
    wo = w_o.astype(BF16)
    win = w_conv_in.astype(BF16)
    wout = w_conv_out.astype(BF16)
    wgu = w_ffn_gu.astype(BF16)
    wd = w_ffn_down.astype(BF16)
    gain = lambda g, i: g[i].reshape(1, D_MODEL)
    lam_vecs = jnp.stack([lambda_q1, lambda_k1, lambda_q2, lambda_k2])
    sub_g = subln_g.reshape(1, V_DIM)
    lam0 = _lambda_init(0)

    xp = x_prompt.reshape(mp, d)
    xs = x_sample.reshape(ms, d)

    g_pre = gain(norm_mix_pre, 0)
    kf_p, vf_p, qb_p, kb_p, vb_p = _qkv_call(xp, g_pre, wqkv, tm_p)
    kf_s, vf_s, qb_s, kb_s, vb_s = _qkv_call(xs, g_pre, wqkv, ms)

    on_p = _prompt_attn_call(qb_p.reshape(b, s, ATTN_WIDTH), kb_p.reshape(b, s, ATTN_WIDTH),
                             vb_p.reshape(b, s, ATTN_WIDTH), rel_bias, lam_vecs, sub_g, lam0, t_attn)

    q4 = qb_s.reshape(nb, dec_seq, N_HEADS, QK_DIM).transpose(0, 2, 1, 3)
    q_rows = jnp.stack([q4, q4], axis=2).reshape(nb, N_HEADS * 2 * dec_seq, QK_DIM)
    pad_tok = PAGE_SIZE // N_HEADS - dec_seq
    flat_new = lambda a: jnp.pad(a.reshape(nb, dec_seq, N_HEADS, QK_DIM),
                                 ((0, 0), (0, pad_tok), (0, 0), (0, 0))).reshape(nb, PAGE_SIZE, QK_DIM)
    on_rows = _sample_attn_call(q_rows, flat_new(kb_s), flat_new(vb_s), cache_k, cache_v, page_table,
                                rel_bias, lam_vecs, sub_g, lam0, dec_seq)
    on_s = on_rows.reshape(nb, N_HEADS, 2, dec_seq, V_DIM)[:, :, 0].transpose(0, 2, 1, 3)
    on_s = on_s.reshape(ms, ATTN_WIDTH).astype(BF16)

    g_post = gain(norm_mix_post, 0)
    xp = _proj_call(xp, on_p.reshape(mp, ATTN_WIDTH), wo, g_post, tm_p)
    xs = _proj_call(xs, on_s, wo, g_post, ms)
    ffn = lambda x, i, tm: _ffn_call(x, gain(norm_ffn_pre, i), wgu[i], wd[i], gain(norm_ffn_post, i), tm)
    xp = ffn(xp, 0, tm_p)
    xs = ffn(xs, 0, ms)

    g_pre, g_post = gain(norm_mix_pre, 1), gain(norm_mix_post, 1)
    zero_state = jnp.zeros((b, CONV_WIDTH - 1, d), F32)
    xp, conv_p = _conv_call(xp, g_pre, win, conv_w, wout, g_post, zero_state, tm_p, s)
    xs, conv_s = _conv_call(xs, g_pre, win, conv_w, wout, g_post, state_conv, ms, dec_seq)
    xp = ffn(xp, 1, tm_p)
    xs = ffn(xs, 1, ms)

    return (xp.reshape(b, s, d), xs.reshape(nb, dec_seq, d),
            kf_p.reshape(b, s, N_HEADS, QK_DIM), vf_p.reshape(b, s, N_HEADS, V_DIM),
            kf_s.reshape(nb, dec_seq, N_HEADS, QK_DIM), vf_s.reshape(nb, dec_seq, N_HEADS, V_DIM),
            conv_p, conv_s)
```

```python
import functools
import math

import jax
import jax.numpy as jnp
from jax import lax
from jax.experimental import pallas as pl
from jax.experimental.pallas import tpu as pltpu

D_MODEL = 1024
N_HEADS = 8
HEAD_DIM = 64
QK_DIM = 2 * HEAD_DIM
V_DIM = 2 * HEAD_DIM
ATTN_WIDTH = N_HEADS * V_DIM
N_BUCKETS = 32
MAX_DISTANCE = 128
CONV_WIDTH = 3
PAGE_SIZE = 128
EPS = 1e-6
NEG_INF = -1e30
SCALE = HEAD_DIM ** -0.5
LOG2E = math.log2(math.e)

V7X_LANES = 128
V7X_SUBLANES = 8
VMEM_LIMIT_BYTES = 56 * 1024 * 1024
SAMPLE_PAGES_PER_STEP = 4

F32 = jnp.float32
BF16 = jnp.bfloat16


def _lambda_init(layer):
    return 0.8 - 0.6 * math.exp(-0.3 * layer)


def _bucket_uppers():
    max_exact = N_BUCKETS // 2
    n_far = None
    buckets = []
    for n in range(0, 4 * MAX_DISTANCE):
        if n < max_exact:
            b = n
        else:
            r = math.log(n / max_exact) / math.log(MAX_DISTANCE / max_exact) * (N_BUCKETS - max_exact)
            frac = r - math.floor(r)
            assert min(frac, 1 - frac) > 1e-4 or n == max_exact or n >= MAX_DISTANCE, n
            b = min(max_exact + int(r + (1e-6 if n == max_exact else 0.0)), N_BUCKETS - 1)
            if n >= MAX_DISTANCE:
                b = N_BUCKETS - 1
        buckets.append(b)
    uppers = []
    for b in range(N_BUCKETS - 1):
        uppers.append(next(n for n, bb in enumerate(buckets) if bb > b))
    n_far = uppers[-1]
    return tuple(uppers), n_far


BUCKET_UPPERS, FAR_DISTANCE = _bucket_uppers()


def _shifted_bias(n, relb_ref, head):
    far = relb_ref[N_BUCKETS - 1, head]
    val = jnp.zeros(n.shape, F32)
    for b in range(N_BUCKETS - 2, -1, -1):
        val = jnp.where(n < BUCKET_UPPERS[b], (relb_ref[b, head] - far) * LOG2E, val)
    return val


def _diff_lambda(lam_ref, lam_init):
    lv = lam_ref[...]
    a = jnp.sum(lv[0:1] * lv[1:2], axis=1, keepdims=True)
    b = jnp.sum(lv[2:3] * lv[3:4], axis=1, keepdims=True)
    return jnp.exp(a) - jnp.exp(b) + lam_init


def _rms(x, g):
    return x * lax.rsqrt(jnp.mean(x * x, axis=-1, keepdims=True) + EPS) * g


def _head_norm(o, g, lam_init):
    return o * lax.rsqrt(jnp.mean(o * o, axis=-1, keepdims=True) + EPS) * g * (1.0 - lam_init)


def _const_spec(shape):
    return pl.BlockSpec(shape, lambda *_: (0,) * len(shape))


def _params(n_grid):
    return pltpu.CompilerParams(dimension_semantics=("arbitrary",) * n_grid,
                                vmem_limit_bytes=VMEM_LIMIT_BYTES)


def _qkv_kernel(x_ref, g_ref, wkv_ref, wqt_ref, wvt_ref, kf_ref, vf_ref, kb_ref, qt_ref, vt_ref):
    hb = _rms(x_ref[...], g_ref[...]).astype(BF16)
    w = ATTN_WIDTH
    tm = hb.shape[0]
    nt = (((1,), (1,)), ((), ()))
    k = jnp.dot(hb, wkv_ref[:, 0:w], preferred_element_type=F32)
    kf_ref[...] = k
    kb_ref[...] = k.astype(BF16)
    vf_ref[...] = jnp.dot(hb, wkv_ref[:, w:2 * w], preferred_element_type=F32)
    qt = lax.dot_general(wqt_ref[...], hb, nt, preferred_element_type=F32)
    qt_ref[...] = (qt * (SCALE * LOG2E)).astype(BF16).reshape(N_HEADS, QK_DIM, tm)
    vt = lax.dot_general(wvt_ref[...], hb, nt, preferred_element_type=F32)
    vt_ref[...] = vt.astype(BF16).reshape(N_HEADS, V_DIM, tm)


def _qkv_call(x, g, wkv_bf16, wqt_bf16, wvt_bf16, tm, seq):
    m = x.shape[0]
    tiles_per_seq = seq // tm
    row = lambda width: pl.BlockSpec((tm, width), lambda i: (i, 0))
    fmaj = pl.BlockSpec((None, N_HEADS, QK_DIM, tm),
                        lambda i: (i // tiles_per_seq, 0, 0, i % tiles_per_seq))
    return pl.pallas_call(
        _qkv_kernel,
        grid=(m // tm,),
        in_specs=[row(D_MODEL), _const_spec((1, D_MODEL)), _const_spec(wkv_bf16.shape),
                  _const_spec(wqt_bf16.shape), _const_spec(wvt_bf16.shape)],
        out_specs=[row(ATTN_WIDTH)] * 3 + [fmaj] * 2,
        out_shape=[jax.ShapeDtypeStruct((m, ATTN_WIDTH), F32)] * 2
        + [jax.ShapeDtypeStruct((m, ATTN_WIDTH), BF16)]
        + [jax.ShapeDtypeStruct((m // seq, N_HEADS, QK_DIM, seq), BF16)] * 2,
        compiler_params=_params(1),
        name="qkv_proj",
    )(x, g, wkv_bf16, wqt_bf16, wvt_bf16)


def _prompt_attn_kernel(relb_ref, lam_ref, g_ref, qt_ref, k_ref, vt_ref, o_ref,
                        q2_ref, m_ref, l_ref, acc_ref, bias_ref, s0_ref, s1_ref, p0_ref, p1_ref,
                        a0_ref, a1_ref, *, t, lam_init):
    head = pl.program_id(1)
    qi = pl.program_id(2)
    s_bufs, p_bufs, a_bufs = (s0_ref, s1_ref), (p0_ref, p1_ref), (a0_ref, a1_ref)

    @pl.when(qi == 0)
    def _build_bias():
        j = lax.broadcasted_iota(jnp.int32, (t, t), 0)
        i = lax.broadcasted_iota(jnp.int32, (t, t), 1)
        bias_ref[0] = jnp.where(j <= i, _shifted_bias(i - j, relb_ref, head), NEG_INF)
        bias_ref[1] = _shifted_bias(i - j + t, relb_ref, head)
        bias_ref[2] = jnp.zeros((t, t), F32)

    qt = qt_ref[...]
    feat = lax.broadcasted_iota(jnp.int32, (QK_DIM, t), 0)
    zero = jnp.zeros_like(qt)
    q2_ref[:, 0:t] = jnp.where(feat < HEAD_DIM, qt, zero)
    q2_ref[:, t:2 * t] = jnp.where(feat >= HEAD_DIM, qt, zero)
    m_ref[...] = jnp.full(m_ref.shape, NEG_INF, F32)
    l_ref[...] = jnp.zeros(l_ref.shape, F32)
    acc_ref[...] = jnp.zeros(acc_ref.shape, F32)
    p0_ref[...] = jnp.zeros(p0_ref.shape, BF16)
    a0_ref[...] = jnp.ones(a0_ref.shape, F32)
    s1_ref[...] = jnp.zeros(s1_ref.shape, F32)

    def time_step(tau, parity, *, scores=True, softmax=True, with_bias=False):
        s_w, s_r = s_bufs[parity], s_bufs[1 - parity]
        p_w, p_r = p_bufs[1 - parity], p_bufs[parity]
        a_w, a_r = a_bufs[1 - parity], a_bufs[parity]
        jv = pl.multiple_of(jnp.clip(tau - 2, 0, qi) * t, t)
        pv = jnp.dot(vt_ref[:, pl.ds(jv, t)], p_r[...], preferred_element_type=F32)
        acc_ref[...] = a_r[...] * acc_ref[...] + pv
        if softmax:
            valid = (tau >= 1) & (tau - 1 <= qi)
            s = s_r[...]
            m_prev = m_ref[...]
            m_new = jnp.where(valid, jnp.maximum(m_prev, jnp.max(s, axis=0, keepdims=True)), m_prev)
            alpha = jnp.exp2(m_prev - m_new)
            p = jnp.exp2(s - jnp.where(valid, m_new, -NEG_INF))
            l_ref[...] = alpha * l_ref[...] + jnp.sum(p, axis=0, keepdims=True)
            p_w[...] = p.astype(BF16)
            a_w[...] = alpha
            m_ref[...] = m_new
        if scores:
            jq = jnp.minimum(tau, qi)
            kt = k_ref[pl.ds(pl.multiple_of(jq * t, t), t), :]
            s_new = jnp.dot(kt, q2_ref[...], preferred_element_type=F32)
            if with_bias:
                b = bias_ref[jnp.minimum(qi - jq, 2)]
                s_new = s_new + jnp.concatenate([b, b], axis=1)
            s_w[...] = s_new

    n_pairs = jnp.maximum(qi - 1, 0) // 2

    def far_pair(u, carry):
        time_step(2 * u, 0)
        time_step(2 * u + 1, 1)
        return carry

    lax.fori_loop(0, n_pairs, far_pair, 0)
    tau0 = 2 * n_pairs
    time_step(tau0, 0, with_bias=True)
    time_step(tau0 + 1, 1, with_bias=True)
    time_step(tau0 + 2, 0, with_bias=True)
    time_step(tau0 + 3, 1, scores=False)
    time_step(tau0 + 4, 0, scores=False, softmax=False)

    inv_l = 1.0 / l_ref[...]
    o1 = acc_ref[:, 0:t] * inv_l[:, 0:t]
    o2 = acc_ref[:, t:2 * t] * inv_l[:, t:2 * t]
    ot = o1 - _diff_lambda(lam_ref, lam_init) * o2
    ms = jnp.mean(ot * ot, axis=0, keepdims=True)
    ont = ot * lax.rsqrt(ms + EPS) * (g_ref[...] * (1.0 - lam_init))
    o_ref[...] = ont.T.astype(BF16)


def _prompt_attn_call(qt, kb, vt, rel_bias, lam_vecs, subln_g_col, lam_init, t):
    b, _, _, s = qt.shape
    assert s % t == 0 and t >= FAR_DISTANCE
    out_tile = pl.BlockSpec((None, t, V_DIM), lambda bi, h, qi: (bi, qi, h))
    q_tile = pl.BlockSpec((None, None, QK_DIM, t), lambda bi, h, qi: (bi, h, 0, qi))
    k_seq = pl.BlockSpec((None, s, QK_DIM), lambda bi, h, qi: (bi, 0, h))
    v_seq = pl.BlockSpec((None, None, V_DIM, s), lambda bi, h, qi: (bi, h, 0, 0))
    return pl.pallas_call(
        functools.partial(_prompt_attn_kernel, t=t, lam_init=lam_init),
        grid=(b, N_HEADS, s // t),
        in_specs=[pl.BlockSpec(memory_space=pltpu.SMEM), _const_spec((4, HEAD_DIM)),
                  _const_spec((V_DIM, 1)), q_tile, k_seq, v_seq],
        out_specs=out_tile,
        out_shape=jax.ShapeDtypeStruct((b, s, ATTN_WIDTH), BF16),
        scratch_shapes=[pltpu.VMEM((QK_DIM, 2 * t), BF16),
                        pltpu.VMEM((1, 2 * t), F32),
                        pltpu.VMEM((1, 2 * t), F32),
                        pltpu.VMEM((V_DIM, 2 * t), F32),
                        pltpu.VMEM((3, t, t), F32),
                        pltpu.VMEM((t, 2 * t), F32), pltpu.VMEM((t, 2 * t), F32),
                        pltpu.VMEM((t, 2 * t), BF16), pltpu.VMEM((t, 2 * t), BF16),
                        pltpu.VMEM((1, 2 * t), F32), pltpu.VMEM((1, 2 * t), F32)],
        compiler_params=_params(3),
        name="prompt_attn",
    )(rel_bias, lam_vecs, subln_g_col, qt, kb, vt)


def _sample_attn_kernel(pt_ref, relb_ref, lam_ref, g_ref, q_ref, kn_ref, vn_ref, *rest,
                        dec_seq, n_steps, group, lam_init):
    del pt_ref
    kc_refs, vc_refs = rest[:group], rest[group:2 * group]
    o_ref, m_ref, l_ref, acc_ref, qm_ref, mask_ref, lastb_ref, newb_ref = rest[2 * group:]
    step = pl.program_id(1)
    rows = N_HEADS * 2 * dec_seq
    per_head = 2 * dec_seq
    assert per_head == V7X_SUBLANES
    flat = PAGE_SIZE * N_HEADS
    nt = (((1,), (1,)), ((), ()))

    def grids(width):
        r = lax.broadcasted_iota(jnp.int32, (rows, width), 0)
        c = lax.broadcasted_iota(jnp.int32, (rows, width), 1)
        same_head = (c % N_HEADS) == (r // per_head)
        return r % dec_seq, c // N_HEADS, same_head

    def head_bias(dist):
        parts = [_shifted_bias(dist[h * per_head:(h + 1) * per_head], relb_ref, h)
                 for h in range(N_HEADS)]
        return jnp.concatenate(parts, axis=0)

    @pl.when((pl.program_id(0) == 0) & (step == 0))
    def _tables():
        tq, tk, ok = grids(flat)
        mask_ref[...] = jnp.where(ok, 0.0, NEG_INF)
        lastb_ref[...] = jnp.where(ok, head_bias(PAGE_SIZE + tq - tk), NEG_INF)
        tqn, tkn, okn = grids(kn_ref.shape[0])
        newb_ref[...] = jnp.where(okn & (tkn <= tqn), head_bias(tqn - tkn), NEG_INF)

    @pl.when(step == 0)
    def _init():
        m_ref[...] = jnp.full(m_ref.shape, NEG_INF, F32)
        l_ref[...] = jnp.zeros(l_ref.shape, F32)
        acc_ref[...] = jnp.zeros(acc_ref.shape, F32)
        q = q_ref[...]
        row = lax.broadcasted_iota(jnp.int32, (rows, QK_DIM), 0)
        lane = lax.broadcasted_iota(jnp.int32, (rows, QK_DIM), 1)
        first_comp = (row // dec_seq) % 2 == 0
        qm_ref[...] = jnp.where((lane < HEAD_DIM) == first_comp, q, jnp.zeros_like(q))

    def update(s_list, v_list):
        m_prev = m_ref[...]
        m_tile = functools.reduce(jnp.maximum, [jnp.max(s, axis=1, keepdims=True) for s in s_list])
        m_new = jnp.maximum(m_prev, m_tile)
        alpha = jnp.exp2(m_prev - m_new)
        p_list = [jnp.exp2(s - m_new[:, 0:1]) for s in s_list]
        l_ref[...] = alpha * l_ref[...] + sum(jnp.sum(p, axis=1, keepdims=True) for p in p_list)
        pv = sum(jnp.dot(p.astype(BF16), v, preferred_element_type=F32)
                 for p, v in zip(p_list, v_list))
        acc_ref[...] = alpha * acc_ref[...] + pv
        m_ref[...] = m_new

    qm = qm_ref[...]
    vals = [vc[...].reshape(flat, V_DIM).astype(BF16) for vc in vc_refs]
    raw = [lax.dot_general(qm, kc[...].reshape(flat, QK_DIM).astype(BF16), nt,
                           preferred_element_type=F32) for kc in kc_refs]

    @pl.when(step < n_steps - 1)
    def _far_pages():
        update([s + mask_ref[...] for s in raw], vals)

    @pl.when(step == n_steps - 1)
    def _last_pages():
        update([s + mask_ref[...] for s in raw[:-1]] + [raw[-1] + lastb_ref[...]], vals)
        sn = lax.dot_general(qm, kn_ref[...], nt, preferred_element_type=F32)
        update([sn + newb_ref[...]], [vn_ref[...]])
        o = acc_ref[...] / l_ref[...]
        o2 = pltpu.roll(o, rows - dec_seq, axis=0)
        od = o - _diff_lambda(lam_ref, lam_init) * o2
        o_ref[...] = _head_norm(od, g_ref[...], lam_init)


def _sample_attn_call(q_rows, k_new, v_new, cache_k, cache_v, page_table, rel_bias, lam_vecs,
                      subln_g, lam_init, dec_seq):
    nb, rows, _ = q_rows.shape
    n_pages = page_table.shape[1]
    group = SAMPLE_PAGES_PER_STEP
    assert PAGE_SIZE >= FAR_DISTANCE + dec_seq and n_pages % group == 0
    n_steps = n_pages // group
    flat = PAGE_SIZE * N_HEADS
    per_b = lambda shape: pl.BlockSpec((None,) + shape, lambda b, p, pt: (b, 0, 0))
    paged = lambda g: pl.BlockSpec((None, PAGE_SIZE, N_HEADS, QK_DIM),
                                   lambda b, p, pt: (pt[b, p * group + g], 0, 0, 0))
    const = lambda shape: pl.BlockSpec(shape, lambda b, p, pt: (0,) * len(shape))
    grid_spec = pltpu.PrefetchScalarGridSpec(
        num_scalar_prefetch=1,
        grid=(nb, n_steps),
        in_specs=[pl.BlockSpec(memory_space=pltpu.SMEM), const((4, HEAD_DIM)), const((1, V_DIM)),
                  per_b((rows, QK_DIM)), per_b(k_new.shape[1:]), per_b(v_new.shape[1:])]
        + [paged(g) for g in range(group)] * 2,
        out_specs=per_b((rows, V_DIM)),
        scratch_shapes=[pltpu.VMEM((rows, V7X_LANES), F32),
                        pltpu.VMEM((rows, V7X_LANES), F32),
                        pltpu.VMEM((rows, V_DIM), F32),
                        pltpu.VMEM((rows, QK_DIM), BF16),
                        pltpu.VMEM((rows, flat), F32),
                        pltpu.VMEM((rows, flat), F32),
                        pltpu.VMEM((rows, k_new.shape[1]), F32)],
    )
    return pl.pallas_call(
        functools.partial(_sample_attn_kernel, dec_seq=dec_seq, n_steps=n_steps, group=group,
                          lam_init=lam_init),
        grid_spec=grid_spec,
        out_shape=jax.ShapeDtypeStruct((nb, rows, V_DIM), F32),
        compiler_params=_params(2),
        name="sample_attn",
    )(page_table, rel_bias, lam_vecs, subln_g, q_rows, k_new, v_new,
      *([cache_k] * group), *([cache_v] * group))


def _proj_kernel(x_ref, a_ref, w_ref, g_ref, o_ref):
    y = jnp.dot(a_ref[...], w_ref[...], preferred_element_type=F32)
    o_ref[...] = x_ref[...] + _rms(y, g_ref[...])


def _proj_call(x, a_bf16, w_bf16, g_post, tm):
    m = x.shape[0]
    row = lambda width: pl.BlockSpec((tm, width), lambda i: (i, 0))
    return pl.pallas_call(
        _proj_kernel,
        grid=(m // tm,),
        in_specs=[row(D_MODEL), row(a_bf16.shape[1]), _const_spec(w_bf16.shape),
                  _const_spec((1, D_MODEL))],
        out_specs=row(D_MODEL),
        out_shape=jax.ShapeDtypeStruct((m, D_MODEL), F32),
        compiler_params=_params(1),
        name="out_proj",
    )(x, a_bf16, w_bf16, g_post)


def _ffn_kernel(x_ref, gpre_ref, wgu_ref, wd_ref, gpost_ref, o_ref, *, d_ff):
    x = x_ref[...]
    hb = _rms(x, gpre_ref[...]).astype(BF16)
    g = jnp.dot(hb, wgu_ref[:, 0:d_ff], preferred_element_type=F32)
    u = jnp.dot(hb, wgu_ref[:, d_ff:2 * d_ff], preferred_element_type=F32)
    a = (g * (1.0 / (1.0 + jnp.exp(-g))) * u).astype(BF16)
    y = jnp.dot(a, wd_ref[...], preferred_element_type=F32)
    o_ref[...] = x + _rms(y, gpost_ref[...])


def _ffn_call(x, g_pre, wgu_bf16, wd_bf16, g_post, tm):
    m = x.shape[0]
    d_ff = wd_bf16.shape[0]
    row = pl.BlockSpec((tm, D_MODEL), lambda i: (i, 0))
    return pl.pallas_call(
        functools.partial(_ffn_kernel, d_ff=d_ff),
        grid=(m // tm,),
        in_specs=[row, _const_spec((1, D_MODEL)), _const_spec(wgu_bf16.shape),
                  _const_spec(wd_bf16.shape), _const_spec((1, D_MODEL))],
        out_specs=row,
        out_shape=jax.ShapeDtypeStruct((m, D_MODEL), F32),
        compiler_params=_params(1),
        name="swiglu_ffn",
    )(x, g_pre, wgu_bf16, wd_bf16, g_post)


def _conv_kernel(*refs, tm, seq):
    carry_mode = seq >= tm
    if carry_mode:
        (x_ref, gpre_ref, win_ref, cw_ref, wout_ref, gpost_ref, st_ref,
         o_ref, tail_ref, carry_ref) = refs
    else:
        (x_ref, gpre_ref, win_ref, cw_ref, wout_ref, gpost_ref, inj1_ref, inj2_ref,
         o_ref, u_ref) = refs
    d = D_MODEL
    x = x_ref[...]
    hb = _rms(x, gpre_ref[...]).astype(BF16)
    gc = jnp.dot(hb, win_ref[:, d:2 * d], preferred_element_type=F32)
    xt = jnp.dot(hb, win_ref[:, 2 * d:3 * d], preferred_element_type=F32)
    u = gc * xt
    row = lax.broadcasted_iota(jnp.int32, (tm, d), 0)
    r1 = pltpu.roll(u, 1, axis=0)
    r2 = pltpu.roll(u, 2, axis=0)
    if carry_mode:
        i = pl.program_id(0)
        tiles_per_seq = seq // tm

        @pl.when(i % tiles_per_seq == 0)
        def _seq_start():
            carry_ref[...] = st_ref[...]

        c = carry_ref[...]
        c1 = c[V7X_SUBLANES - 1:V7X_SUBLANES]
        c2 = c[V7X_SUBLANES - 2:V7X_SUBLANES - 1]
        prev1 = jnp.where(row == 0, c1, r1)
        prev2 = jnp.where(row == 0, c2, jnp.where(row == 1, c1, r2))
        tail = u[tm - V7X_SUBLANES:tm]
        carry_ref[...] = tail
        tail_ref[...] = tail
    else:
        t = row % seq
        prev1 = jnp.where(t >= 1, r1, inj1_ref[...])
        prev2 = jnp.where(t >= 2, r2, inj2_ref[...])
        u_ref[...] = u
    cw = cw_ref[...]
    y = cw[0:1] * prev2 + cw[1:2] * prev1 + cw[2:3] * u
    gb = jnp.dot(hb, win_ref[:, 0:d], preferred_element_type=F32)
    z = jnp.dot((gb * y).astype(BF16), wout_ref[...], preferred_element_type=F32)
    o_ref[...] = x + _rms(z, gpost_ref[...])


def _conv_call(x, g_pre, win_bf16, conv_w, wout_bf16, g_post, state, tm, seq):
    m = x.shape[0]
    nb = m // seq
    row = pl.BlockSpec((tm, D_MODEL), lambda i: (i, 0))
    common = [row, _const_spec((1, D_MODEL)), _const_spec(win_bf16.shape),
              _const_spec((CONV_WIDTH, D_MODEL)), _const_spec(wout_bf16.shape),
              _const_spec((1, D_MODEL))]
    kern = functools.partial(_conv_kernel, tm=tm, seq=seq)
    if seq >= tm:
        assert seq % tm == 0
        n_tiles = m // tm
        tiles_per_seq = seq // tm
        pad = jnp.zeros((nb, V7X_SUBLANES - (CONV_WIDTH - 1), D_MODEL), F32)
        st = jnp.concatenate([pad, state], axis=1)
        st_spec = pl.BlockSpec((None, V7X_SUBLANES, D_MODEL), lambda i: (i // tiles_per_seq, 0, 0))
        tail_spec = pl.BlockSpec((None, V7X_SUBLANES, D_MODEL), lambda i: (i, 0, 0))
        out, tails = pl.pallas_call(
            kern, grid=(n_tiles,),
            in_specs=common + [st_spec],
            out_specs=[row, tail_spec],
            out_shape=[jax.ShapeDtypeStruct((m, D_MODEL), F32),
                       jax.ShapeDtypeStruct((n_tiles, V7X_SUBLANES, D_MODEL), F32)],
            scratch_shapes=[pltpu.VMEM((V7X_SUBLANES, D_MODEL), F32)],
            compiler_params=_params(1),
            name="short_conv_seq",
        )(x, g_pre, win_bf16, conv_w, wout_bf16, g_post, st)
        last = tails.reshape(nb, tiles_per_seq, V7X_SUBLANES, D_MODEL)[:, -1]
        return out, last[:, V7X_SUBLANES - (CONV_WIDTH - 1):]
    assert tm % seq == 0 and m == tm and seq >= CONV_WIDTH - 1
    zeros = lambda n: jnp.zeros((nb, n, D_MODEL), F32)
    inj1 = jnp.concatenate([state[:, 1:2], zeros(seq - 1)], axis=1).reshape(m, D_MODEL)
    inj2 = jnp.concatenate([state[:, 0:2], zeros(seq - 2)], axis=1).reshape(m, D_MODEL)
    out, u = pl.pallas_call(
        kern, grid=(1,),
        in_specs=common + [row, row],
        out_specs=[row, row],
        out_shape=[jax.ShapeDtypeStruct((m, D_MODEL), F32)] * 2,
        compiler_params=_params(1),
        name="short_conv_tok",
    )(x, g_pre, win_bf16, conv_w, wout_bf16, g_post, inj1, inj2)
    return out, u.reshape(nb, seq, D_MODEL)[:, seq - (CONV_WIDTH - 1):]


def kernel(x_prompt, x_sample, cache_k, cache_v, state_conv, page_table, rel_bias, w_qkv, w_o, lambda_q1, lambda_k1, lambda_q2, lambda_k2, subln_g, w_conv_in, conv_w, w_conv_out, norm_mix_pre, norm_mix_post, norm_ffn_pre, norm_ffn_post, w_ffn_gu, w_ffn_down):
    b, s, d = x_prompt.shape
    nb, dec_seq, _ = x_sample.shape
    assert d == D_MODEL and page_table.shape[1] * PAGE_SIZE >= FAR_DISTANCE
    mp, ms = b * s, nb * dec_seq
    tm_p, t_attn = 256, 256

    wkv = w_qkv[:, ATTN_WIDTH:].astype(BF16)
    wqt = w_qkv[:, :ATTN_WIDTH].T.astype(BF16)
    wvt = w_qkv[:, 2 * ATTN_WIDTH:].T.astype(BF16)
    wo = w_o.astype(BF16)
    win = w_conv_in.astype(BF16)
    wout = w_conv_out.astype(BF16)
    wgu = w_ffn_gu.astype(BF16)
    wd = w_ffn_down.astype(BF16)
    gain = lambda g, i: g[i].reshape(1, D_MODEL)
    lam_vecs = jnp.stack([lambda_q1, lambda_k1, lambda_q2, lambda_k2])
    sub_g = subln_g.reshape(1, V_DIM)
    lam0 = _lambda_init(0)

    xp = x_prompt.reshape(mp, d)
    xs = x_sample.reshape(ms, d)

    g_pre = gain(norm_mix_pre, 0)
    kf_p, vf_p, kb_p, qt_p, vt_p = _qkv_call(xp, g_pre, wkv, wqt, wvt, tm_p, s)
    kf_s, vf_s, kb_s, qt_s, _ = _qkv_call(xs, g_pre, wkv, wqt, wvt, ms, ms)

    on_p = _prompt_attn_call(qt_p, kb_p.reshape(b, s, ATTN_WIDTH), vt_p, rel_bias, lam_vecs,
                             subln_g.reshape(V_DIM, 1), lam0, t_attn)

    q4 = qt_s.reshape(N_HEADS, QK_DIM, nb, dec_seq).transpose(2, 0, 3, 1)
    q_rows = jnp.stack([q4, q4], axis=2).reshape(nb, N_HEADS * 2 * dec_seq, QK_DIM)
    pad_tok = PAGE_SIZE // N_HEADS - dec_seq
    flat_new = lambda a: jnp.pad(a.reshape(nb, dec_seq, N_HEADS, QK_DIM),
                                 ((0, 0), (0, pad_tok), (0, 0), (0, 0))).reshape(nb, PAGE_SIZE, QK_DIM)
    on_rows = _sample_attn_call(q_rows, flat_new(kb_s), flat_new(vf_s.astype(BF16)), cache_k, cache_v,
                                page_table, rel_bias, lam_vecs, sub_g, lam0, dec_seq)
    on_s = on_rows.reshape(nb, N_HEADS, 2, dec_seq, V_DIM)[:, :, 0].transpose(0, 2, 1, 3)
    on_s = on_s.reshape(ms, ATTN_WIDTH).astype(BF16)

    g_post = gain(norm_mix_post, 0)
    xp = _proj_call(xp, on_p.reshape(mp, ATTN_WIDTH), wo, g_post, tm_p)
    xs = _proj_call(xs, on_s, wo, g_post, ms)
    ffn = lambda x, i, tm: _ffn_call(x, gain(norm_ffn_pre, i), wgu[i], wd[i], gain(norm_ffn_post, i), tm)
    xp = ffn(xp, 0, tm_p)
    xs = ffn(xs, 0, ms)

    g_pre, g_post = gain(norm_mix_pre, 1), gain(norm_mix_post, 1)
    zero_state = jnp.zeros((b, CONV_WIDTH - 1, d), F32)
    xp, conv_p = _conv_call(xp, g_pre, win, conv_w, wout, g_post, zero_state, tm_p, s)
    xs, conv_s = _conv_call(xs, g_pre, win, conv_w, wout, g_post, state_conv, ms, dec_seq)
    xp = ffn(xp, 1, tm_p)
    xs = ffn(xs, 1, ms)

    return (xp.reshape(b, s, d), xs.reshape(nb, dec_seq, d),
            kf_p.reshape(b, s, N_HEADS, QK_DIM), vf_p.reshape(b, s, N_HEADS, V_DIM),
            kf_s.reshape(nb, dec_seq, N_HEADS, QK_DIM), vf_s.reshape(nb, dec_seq, N_HEADS, V_DIM),
            conv_p, conv_s)
```

```python
import functools
import math

import jax
import jax.numpy as jnp
from jax import lax
from jax.experimental import pallas as pl
from jax.experimental.pallas import tpu as pltpu

D_MODEL = 1024
N_HEADS = 8
HEAD_DIM = 64
QK_DIM = 2 * HEAD_DIM
V_DIM = 2 * HEAD_DIM
ATTN_WIDTH = N_HEADS * V_DIM
N_BUCKETS = 32
MAX_DISTANCE = 128
CONV_WIDTH = 3
PAGE_SIZE = 128
EPS = 1e-6
NEG_INF = -1e30
SCALE = HEAD_DIM ** -0.5
LOG2E = math.log2(math.e)

V7X_LANES = 128
V7X_BF16_SUBLANES = 16
V_EXT = V_DIM + V7X_BF16_SUBLANES
V7X_SUBLANES = 8
VMEM_LIMIT_BYTES = 56 * 1024 * 1024
SAMPLE_PAGES_PER_STEP = 8
TOKEN_TILE = 256
ATTN_QUERY_TILE = 512
ATTN_KEY_TILE = 256

F32 = jnp.float32
BF16 = jnp.bfloat16


def _lambda_init(layer):
    return 0.8 - 0.6 * math.exp(-0.3 * layer)


def _bucket_uppers():
    max_exact = N_BUCKETS // 2
    n_far = None
    buckets = []
    for n in range(0, 4 * MAX_DISTANCE):
        if n < max_exact:
            b = n
        else:
            r = math.log(n / max_exact) / math.log(MAX_DISTANCE / max_exact) * (N_BUCKETS - max_exact)
            frac = r - math.floor(r)
            assert min(frac, 1 - frac) > 1e-4 or n == max_exact or n >= MAX_DISTANCE, n
            b = min(max_exact + int(r + (1e-6 if n == max_exact else 0.0)), N_BUCKETS - 1)
            if n >= MAX_DISTANCE:
                b = N_BUCKETS - 1
        buckets.append(b)
    uppers = []
    for b in range(N_BUCKETS - 1):
        uppers.append(next(n for n, bb in enumerate(buckets) if bb > b))
    n_far = uppers[-1]
    return tuple(uppers), n_far


BUCKET_UPPERS, FAR_DISTANCE = _bucket_uppers()


def _shifted_bias(n, relb_ref, head):
    far = relb_ref[N_BUCKETS - 1, head]
    val = jnp.zeros(n.shape, F32)
    for b in range(N_BUCKETS - 2, -1, -1):
        val = jnp.where(n < BUCKET_UPPERS[b], (relb_ref[b, head] - far) * LOG2E, val)
    return val


def _diff_lambda(lam_ref, lam_init):
    lv = lam_ref[...]
    a = jnp.sum(lv[0:1] * lv[1:2], axis=1, keepdims=True)
    b = jnp.sum(lv[2:3] * lv[3:4], axis=1, keepdims=True)
    return jnp.exp(a) - jnp.exp(b) + lam_init


def _rms(x, g):
    return x * lax.rsqrt(jnp.mean(x * x, axis=-1, keepdims=True) + EPS) * g


def _head_norm(o, g, lam_init):
    return o * lax.rsqrt(jnp.mean(o * o, axis=-1, keepdims=True) + EPS) * g * (1.0 - lam_init)


def _const_spec(shape):
    return pl.BlockSpec(shape, lambda *_: (0,) * len(shape))


def _params(n_grid):
    return pltpu.CompilerParams(dimension_semantics=("arbitrary",) * n_grid,
                                vmem_limit_bytes=VMEM_LIMIT_BYTES)


def _qkv_kernel(x_ref, g_ref, wkv_ref, wqt_ref, wvt_ref, kf_ref, vf_ref, kb_ref, qt_ref, vt_ref):
    hb = _rms(x_ref[...], g_ref[...]).astype(BF16)
    w = ATTN_WIDTH
    tm = hb.shape[0]
    nt = (((1,), (1,)), ((), ()))
    k = jnp.dot(hb, wkv_ref[:, 0:w], preferred_element_type=F32)
    kf_ref[...] = k
    kb_ref[...] = k.astype(BF16)
    vf_ref[...] = jnp.dot(hb, wkv_ref[:, w:2 * w], preferred_element_type=F32)
    qt = lax.dot_general(wqt_ref[...], hb, nt, preferred_element_type=F32)
    qt_ref[...] = (qt * (SCALE * LOG2E)).astype(BF16).reshape(N_HEADS, QK_DIM, tm)
    vt = lax.dot_general(wvt_ref[...], hb, nt, preferred_element_type=F32)
    vt_ref[:, 0:V_DIM, :] = vt.astype(BF16).reshape(N_HEADS, V_DIM, tm)
    vt_ref[:, V_DIM:V_EXT, :] = jnp.ones((N_HEADS, V_EXT - V_DIM, tm), BF16)


def _qkv_call(x, g, wkv_bf16, wqt_bf16, wvt_bf16, tm, seq):
    m = x.shape[0]
    tiles_per_seq = seq // tm
    row = lambda width: pl.BlockSpec((tm, width), lambda i: (i, 0))
    fmaj = lambda rows: pl.BlockSpec((None, N_HEADS, rows, tm),
                                     lambda i: (i // tiles_per_seq, 0, 0, i % tiles_per_seq))
    return pl.pallas_call(
        _qkv_kernel,
        grid=(m // tm,),
        in_specs=[row(D_MODEL), _const_spec((1, D_MODEL)), _const_spec(wkv_bf16.shape),
                  _const_spec(wqt_bf16.shape), _const_spec(wvt_bf16.shape)],
        out_specs=[row(ATTN_WIDTH)] * 3 + [fmaj(QK_DIM), fmaj(V_EXT)],
        out_shape=[jax.ShapeDtypeStruct((m, ATTN_WIDTH), F32)] * 2
        + [jax.ShapeDtypeStruct((m, ATTN_WIDTH), BF16)]
        + [jax.ShapeDtypeStruct((m // seq, N_HEADS, QK_DIM, seq), BF16),
           jax.ShapeDtypeStruct((m // seq, N_HEADS, V_EXT, seq), BF16)],
        compiler_params=_params(1),
        name="qkv_proj",
    )(x, g, wkv_bf16, wqt_bf16, wvt_bf16)


def _prompt_attn_kernel(relb_ref, lam_ref, g_ref, qt_ref, k_ref, vt_ref, o_ref,
                        q2_ref, m_ref, acc_ref, bias_ref, s0_ref, s1_ref, p0_ref, p1_ref,
                        a0_ref, a1_ref, t0_ref, t1_ref, *, tq, tk, lam_init):
    head = pl.program_id(0)
    qi = pl.program_id(1)
    chains = range(qt_ref.shape[0])
    ratio = tq // tk
    n_band = ratio + 1
    last = ratio * (qi + 1) - 1

    @pl.when(qi == 0)
    def _build_bias():
        j = lax.broadcasted_iota(jnp.int32, (tk, tq), 0)
        i = lax.broadcasted_iota(jnp.int32, (tk, tq), 1)
        for d in range(n_band):
            n = i - j + tk * (d + 1 - ratio)
            bias_ref[d] = jnp.where(n >= 0, _shifted_bias(n, relb_ref, head), NEG_INF)
        bias_ref[n_band] = jnp.zeros((tk, tq), F32)

    feat = lax.broadcasted_iota(jnp.int32, (QK_DIM, tq), 0)
    for c in chains:
        qt = qt_ref[c]
        zero = jnp.zeros_like(qt)
        q2_ref[c, :, 0:tq] = jnp.where(feat < HEAD_DIM, qt, zero)
        q2_ref[c, :, tq:2 * tq] = jnp.where(feat >= HEAD_DIM, qt, zero)
    m_ref[...] = jnp.full(m_ref.shape, NEG_INF, F32)
    acc_ref[...] = jnp.zeros(acc_ref.shape, F32)
    for p_buf, a_buf in ((p0_ref, a0_ref), (p1_ref, a1_ref)):
        p_buf[...] = jnp.zeros(p_buf.shape, BF16)
        a_buf[...] = jnp.ones(a_buf.shape, F32)

    @pl.when((head == 0) & (qi == 0))
    def _first_touch():
        for s_buf, t_buf in ((s0_ref, t0_ref), (s1_ref, t1_ref)):
            s_buf[...] = jnp.zeros(s_buf.shape, F32)
            t_buf[...] = jnp.zeros(t_buf.shape, F32)

    def time_step(tau, parity, *, scores=True, softmax=True, with_bias=False):
        s_buf, t_buf = (s0_ref, t0_ref) if parity == 0 else (s1_ref, t1_ref)
        p_buf, a_buf = (p0_ref, a0_ref) if parity == 0 else (p1_ref, a1_ref)
        jv = pl.multiple_of(jnp.clip(tau - 4, 0, last) * tk, tk)
        valid = (tau >= 2) & (tau - 2 <= last)
        jq = jnp.minimum(tau, last)
        start = pl.multiple_of(jq * tk, tk)
        if scores and with_bias:
            b = bias_ref[jnp.minimum(last - jq, n_band)]
            b2 = jnp.concatenate([b, b], axis=1)

        def values_stage(c):
            pv = jnp.dot(vt_ref[c, :, pl.ds(jv, tk)], p_buf[c], preferred_element_type=F32)
            acc_ref[c] = a_buf[c] * acc_ref[c] + pv

        def softmax_stage(c):
            m_prev = m_ref[c]
            m_new = jnp.where(valid, jnp.maximum(m_prev, t_buf[c]), m_prev)
            a_buf[c] = jnp.exp2(m_prev - m_new)
            p_buf[c] = jnp.exp2((s_buf[c] - jnp.where(valid, m_new, -NEG_INF)).astype(BF16))
            m_ref[c] = m_new

        def scores_stage(c):
            s_new = jnp.dot(k_ref[c, pl.ds(start, tk), :], q2_ref[c],
                            preferred_element_type=F32)
            if with_bias:
                s_new = s_new + b2
            s_buf[c] = s_new
            t_buf[c] = jnp.max(s_new, axis=0, keepdims=True)

        for c in chains:
            values_stage(c)
            if softmax:
                softmax_stage(c)
            if scores:
                scores_stage(c)

    n_far = jnp.maximum(last + 1 - n_band, 0)
    n_quads = n_far // 4
    n_pairs = n_far // 2

    def far_quad(u, carry):
        for d in range(4):
            time_step(4 * u + d, d % 2)
        return carry

    def far_pair(u, carry):
        time_step(2 * u, 0)
        time_step(2 * u + 1, 1)
        return carry

    lax.fori_loop(0, n_quads, far_quad, 0)
    lax.fori_loop(2 * n_quads, n_pairs, far_pair, 0)
    tau0 = 2 * n_pairs
    n_tail = n_band + 1
    for d in range(n_tail + 4):
        time_step(tau0 + d, d % 2, scores=d < n_tail, softmax=d < n_tail + 2, with_bias=True)

    lam = _diff_lambda(lam_ref, lam_init)
    gain = g_ref[...] * (1.0 - lam_init)
    for c in chains:
        inv_l = 1.0 / acc_ref[c, V_DIM:V_DIM + 1, :]
        o1 = acc_ref[c, 0:V_DIM, 0:tq] * inv_l[:, 0:tq]
        o2 = acc_ref[c, 0:V_DIM, tq:2 * tq] * inv_l[:, tq:2 * tq]
        ot = o1 - lam * o2
        ms = jnp.mean(ot * ot, axis=0, keepdims=True)
        o_ref[c] = (ot * lax.rsqrt(ms + EPS) * gain).T.astype(BF16)


def _prompt_attn_call(qt, kb, vt, rel_bias, lam_vecs, subln_g_col, lam_init, tq, tk):
    b, _, _, s = qt.shape
    assert s % tq == 0 and tq % tk == 0 and tk >= FAR_DISTANCE
    out_tile = pl.BlockSpec((b, tq, V_DIM), lambda h, qi: (0, qi, h))
    q_tile = pl.BlockSpec((b, None, QK_DIM, tq), lambda h, qi: (0, h, 0, qi))
    k_seq = pl.BlockSpec((b, s, QK_DIM), lambda h, qi: (0, 0, h))
    v_seq = pl.BlockSpec((b, None, V_EXT, s), lambda h, qi: (0, h, 0, 0))
    stat = pltpu.VMEM((b, 1, 2 * tq), F32)
    return pl.pallas_call(
        functools.partial(_prompt_attn_kernel, tq=tq, tk=tk, lam_init=lam_init),
        grid=(N_HEADS, s // tq),
        in_specs=[pl.BlockSpec(memory_space=pltpu.SMEM), _const_spec((4, HEAD_DIM)),
                  _const_spec((V_DIM, 1)), q_tile, k_seq, v_seq],
        out_specs=out_tile,
        out_shape=jax.ShapeDtypeStruct((b, s, ATTN_WIDTH), BF16),
        scratch_shapes=[pltpu.VMEM((b, QK_DIM, 2 * tq), BF16), stat,
                        pltpu.VMEM((b, V_EXT, 2 * tq), F32),
                        pltpu.VMEM((tq // tk + 2, tk, tq), F32),
                        pltpu.VMEM((b, tk, 2 * tq), F32), pltpu.VMEM((b, tk, 2 * tq), F32),
                        pltpu.VMEM((b, tk, 2 * tq), BF16), pltpu.VMEM((b, tk, 2 * tq), BF16),
                        stat, stat, stat, stat],
        compiler_params=_params(2),
        name="prompt_attn",
    )(rel_bias, lam_vecs, subln_g_col, qt, kb, vt)


def _sample_attn_kernel(pt_ref, relb_ref, lam_ref, g_ref, q_ref, kn_ref, vn_ref, *rest,
                        dec_seq, n_steps, group, lam_init):
    del pt_ref
    kc_refs, vc_refs = rest[:group], rest[group:2 * group]
    o_ref, m_ref, l_ref, acc_ref, qm_ref, mask_ref, lastb_ref, newb_ref = rest[2 * group:]
    step = pl.program_id(1)
    rows = N_HEADS * 2 * dec_seq
    per_head = 2 * dec_seq
    assert per_head == V7X_SUBLANES
    flat = PAGE_SIZE * N_HEADS
    nt = (((1,), (1,)), ((), ()))

    def grids(width):
        r = lax.broadcasted_iota(jnp.int32, (rows, width), 0)
        c = lax.broadcasted_iota(jnp.int32, (rows, width), 1)
        same_head = (c % N_HEADS) == (r // per_head)
        return r % dec_seq, c // N_HEADS, same_head

    def head_bias(dist):
        parts = [_shifted_bias(dist[h * per_head:(h + 1) * per_head], relb_ref, h)
                 for h in range(N_HEADS)]
        return jnp.concatenate(parts, axis=0)

    @pl.when((pl.program_id(0) == 0) & (step == 0))
    def _tables():
        tq, tk, ok = grids(flat)
        mask_ref[...] = jnp.where(ok, 0.0, NEG_INF)
        lastb_ref[...] = jnp.where(ok, head_bias(PAGE_SIZE + tq - tk), NEG_INF)
        tqn, tkn, okn = grids(kn_ref.shape[0])
        newb_ref[...] = jnp.where(okn & (tkn <= tqn), head_bias(tqn - tkn), NEG_INF)

    @pl.when(step == 0)
    def _init():
        m_ref[...] = jnp.full(m_ref.shape, NEG_INF, F32)
        l_ref[...] = jnp.zeros(l_ref.shape, F32)
        acc_ref[...] = jnp.zeros(acc_ref.shape, F32)
        q = q_ref[...]
        row = lax.broadcasted_iota(jnp.int32, (rows, QK_DIM), 0)
        lane = lax.broadcasted_iota(jnp.int32, (rows, QK_DIM), 1)
        first_comp = (row // dec_seq) % 2 == 0
        qm_ref[...] = jnp.where((lane < HEAD_DIM) == first_comp, q, jnp.zeros_like(q))

    def update(s_list, v_list):
        m_prev = m_ref[...]
        m_tile = functools.reduce(jnp.maximum, [jnp.max(s, axis=1, keepdims=True) for s in s_list])
        m_new = jnp.maximum(m_prev, m_tile)
        alpha = jnp.exp2(m_prev - m_new)
        p_list = [jnp.exp2(s - m_new[:, 0:1]) for s in s_list]
        l_ref[...] = alpha * l_ref[...] + sum(jnp.sum(p, axis=1, keepdims=True) for p in p_list)
        pv = sum(jnp.dot(p.astype(BF16), v, preferred_element_type=F32)
                 for p, v in zip(p_list, v_list))
        acc_ref[...] = alpha * acc_ref[...] + pv
        m_ref[...] = m_new

    qm = qm_ref[...]
    vals = [vc[...].reshape(flat, V_DIM).astype(BF16) for vc in vc_refs]
    raw = [lax.dot_general(qm, kc[...].reshape(flat, QK_DIM).astype(BF16), nt,
                           preferred_element_type=F32) for kc in kc_refs]

    @pl.when(step < n_steps - 1)
    def _far_pages():
        update([s + mask_ref[...] for s in raw], vals)

    @pl.when(step == n_steps - 1)
    def _last_pages():
        update([s + mask_ref[...] for s in raw[:-1]] + [raw[-1] + lastb_ref[...]], vals)
        sn = lax.dot_general(qm, kn_ref[...], nt, preferred_element_type=F32)
        update([sn + newb_ref[...]], [vn_ref[...]])
        o = acc_ref[...] / l_ref[...]
        o2 = pltpu.roll(o, rows - dec_seq, axis=0)
        od = o - _diff_lambda(lam_ref, lam_init) * o2
        o_ref[...] = _head_norm(od, g_ref[...], lam_init)


def _sample_attn_call(q_rows, k_new, v_new, cache_k, cache_v, page_table, rel_bias, lam_vecs,
                      subln_g, lam_init, dec_seq):
    nb, rows, _ = q_rows.shape
    n_pages = page_table.shape[1]
    group = SAMPLE_PAGES_PER_STEP
    assert PAGE_SIZE >= FAR_DISTANCE + dec_seq and n_pages % group == 0
    n_steps = n_pages // group
    flat = PAGE_SIZE * N_HEADS
    per_b = lambda shape: pl.BlockSpec((None,) + shape, lambda b, p, pt: (b, 0, 0))
    paged = lambda g: pl.BlockSpec((None, PAGE_SIZE, N_HEADS, QK_DIM),
                                   lambda b, p, pt: (pt[b, p * group + g], 0, 0, 0))
    const = lambda shape: pl.BlockSpec(shape, lambda b, p, pt: (0,) * len(shape))
    grid_spec = pltpu.PrefetchScalarGridSpec(
        num_scalar_prefetch=1,
        grid=(nb, n_steps),
        in_specs=[pl.BlockSpec(memory_space=pltpu.SMEM), const((4, HEAD_DIM)), const((1, V_DIM)),
                  per_b((rows, QK_DIM)), per_b(k_new.shape[1:]), per_b(v_new.shape[1:])]
        + [paged(g) for g in range(group)] * 2,
        out_specs=per_b((rows, V_DIM)),
        scratch_shapes=[pltpu.VMEM((rows, V7X_LANES), F32),
                        pltpu.VMEM((rows, V7X_LANES), F32),
                        pltpu.VMEM((rows, V_DIM), F32),
                        pltpu.VMEM((rows, QK_DIM), BF16),
                        pltpu.VMEM((rows, flat), F32),
                        pltpu.VMEM((rows, flat), F32),
                        pltpu.VMEM((rows, k_new.shape[1]), F32)],
    )
    return pl.pallas_call(
        functools.partial(_sample_attn_kernel, dec_seq=dec_seq, n_steps=n_steps, group=group,
                          lam_init=lam_init),
        grid_spec=grid_spec,
        out_shape=jax.ShapeDtypeStruct((nb, rows, V_DIM), F32),
        compiler_params=_params(2),
        name="sample_attn",
    )(page_table, rel_bias, lam_vecs, subln_g, q_rows, k_new, v_new,
      *([cache_k] * group), *([cache_v] * group))


def _proj_kernel(x_ref, a_ref, w_ref, g_ref, o_ref):
    y = jnp.dot(a_ref[...], w_ref[...], preferred_element_type=F32)
    o_ref[...] = x_ref[...] + _rms(y, g_ref[...])


def _proj_call(x, a_bf16, w_bf16, g_post, tm):
    m = x.shape[0]
    row = lambda width: pl.BlockSpec((tm, width), lambda i: (i, 0))
    return pl.pallas_call(
        _proj_kernel,
        grid=(m // tm,),
        in_specs=[row(D_MODEL), row(a_bf16.shape[1]), _const_spec(w_bf16.shape),
                  _const_spec((1, D_MODEL))],
        out_specs=row(D_MODEL),
        out_shape=jax.ShapeDtypeStruct((m, D_MODEL), F32),
        compiler_params=_params(1),
        name="out_proj",
    )(x, a_bf16, w_bf16, g_post)


def _ffn_kernel(x_ref, gpre_ref, wgu_ref, wd_ref, gpost_ref, o_ref, *, d_ff):
    x = x_ref[...]
    hb = _rms(x, gpre_ref[...]).astype(BF16)
    g = jnp.dot(hb, wgu_ref[:, 0:d_ff], preferred_element_type=F32)
    u = jnp.dot(hb, wgu_ref[:, d_ff:2 * d_ff], preferred_element_type=F32)
    a = (g * (1.0 / (1.0 + jnp.exp(-g))) * u).astype(BF16)
    y = jnp.dot(a, wd_ref[...], preferred_element_type=F32)
    o_ref[...] = x + _rms(y, gpost_ref[...])


def _ffn_call(x, g_pre, wgu_bf16, wd_bf16, g_post, tm):
    m = x.shape[0]
    d_ff = wd_bf16.shape[0]
    row = pl.BlockSpec((tm, D_MODEL), lambda i: (i, 0))
    return pl.pallas_call(
        functools.partial(_ffn_kernel, d_ff=d_ff),
        grid=(m // tm,),
        in_specs=[row, _const_spec((1, D_MODEL)), _const_spec(wgu_bf16.shape),
                  _const_spec(wd_bf16.shape), _const_spec((1, D_MODEL))],
        out_specs=row,
        out_shape=jax.ShapeDtypeStruct((m, D_MODEL), F32),
        compiler_params=_params(1),
        name="swiglu_ffn",
    )(x, g_pre, wgu_bf16, wd_bf16, g_post)


def _conv_kernel(*refs, tm, seq):
    carry_mode = seq >= tm
    if carry_mode:
        (x_ref, gpre_ref, win_ref, cw_ref, wout_ref, gpost_ref, st_ref,
         o_ref, tail_ref, carry_ref) = refs
    else:
        (x_ref, gpre_ref, win_ref, cw_ref, wout_ref, gpost_ref, inj1_ref, inj2_ref,
         o_ref, u_ref) = refs
    d = D_MODEL
    x = x_ref[...]
    hb = _rms(x, gpre_ref[...]).astype(BF16)
    gc = jnp.dot(hb, win_ref[:, d:2 * d], preferred_element_type=F32)
    xt = jnp.dot(hb, win_ref[:, 2 * d:3 * d], preferred_element_type=F32)
    u = gc * xt
    row = lax.broadcasted_iota(jnp.int32, (tm, d), 0)
    r1 = pltpu.roll(u, 1, axis=0)
    r2 = pltpu.roll(u, 2, axis=0)
    if carry_mode:
        i = pl.program_id(0)
        tiles_per_seq = seq // tm

        @pl.when(i % tiles_per_seq == 0)
        def _seq_start():
            carry_ref[...] = st_ref[...]

        c = carry_ref[...]
        c1 = c[V7X_SUBLANES - 1:V7X_SUBLANES]
        c2 = c[V7X_SUBLANES - 2:V7X_SUBLANES - 1]
        prev1 = jnp.where(row == 0, c1, r1)
        prev2 = jnp.where(row == 0, c2, jnp.where(row == 1, c1, r2))
        tail = u[tm - V7X_SUBLANES:tm]
        carry_ref[...] = tail
        tail_ref[...] = tail
    else:
        t = row % seq
        prev1 = jnp.where(t >= 1, r1, inj1_ref[...])
        prev2 = jnp.where(t >= 2, r2, inj2_ref[...])
        u_ref[...] = u
    cw = cw_ref[...]
    y = cw[0:1] * prev2 + cw[1:2] * prev1 + cw[2:3] * u
    gb = jnp.dot(hb, win_ref[:, 0:d], preferred_element_type=F32)
    z = jnp.dot((gb * y).astype(BF16), wout_ref[...], preferred_element_type=F32)
    o_ref[...] = x + _rms(z, gpost_ref[...])


def _conv_call(x, g_pre, win_bf16, conv_w, wout_bf16, g_post, state, tm, seq):
    m = x.shape[0]
    nb = m // seq
    row = pl.BlockSpec((tm, D_MODEL), lambda i: (i, 0))
    common = [row, _const_spec((1, D_MODEL)), _const_spec(win_bf16.shape),
              _const_spec((CONV_WIDTH, D_MODEL)), _const_spec(wout_bf16.shape),
              _const_spec((1, D_MODEL))]
    kern = functools.partial(_conv_kernel, tm=tm, seq=seq)
    if seq >= tm:
        assert seq % tm == 0
        n_tiles = m // tm
        tiles_per_seq = seq // tm
        pad = jnp.zeros((nb, V7X_SUBLANES - (CONV_WIDTH - 1), D_MODEL), F32)
        st = jnp.concatenate([pad, state], axis=1)
        st_spec = pl.BlockSpec((None, V7X_SUBLANES, D_MODEL), lambda i: (i // tiles_per_seq, 0, 0))
        tail_spec = pl.BlockSpec((None, V7X_SUBLANES, D_MODEL), lambda i: (i, 0, 0))
        out, tails = pl.pallas_call(
            kern, grid=(n_tiles,),
            in_specs=common + [st_spec],
            out_specs=[row, tail_spec],
            out_shape=[jax.ShapeDtypeStruct((m, D_MODEL), F32),
                       jax.ShapeDtypeStruct((n_tiles, V7X_SUBLANES, D_MODEL), F32)],
            scratch_shapes=[pltpu.VMEM((V7X_SUBLANES, D_MODEL), F32)],
            compiler_params=_params(1),
            name="short_conv_seq",
        )(x, g_pre, win_bf16, conv_w, wout_bf16, g_post, st)
        last = tails.reshape(nb, tiles_per_seq, V7X_SUBLANES, D_MODEL)[:, -1]
        return out, last[:, V7X_SUBLANES - (CONV_WIDTH - 1):]
    assert tm % seq == 0 and m == tm and seq >= CONV_WIDTH - 1
    zeros = lambda n: jnp.zeros((nb, n, D_MODEL), F32)
    inj1 = jnp.concatenate([state[:, 1:2], zeros(seq - 1)], axis=1).reshape(m, D_MODEL)
    inj2 = jnp.concatenate([state[:, 0:2], zeros(seq - 2)], axis=1).reshape(m, D_MODEL)
    out, u = pl.pallas_call(
        kern, grid=(1,),
        in_specs=common + [row, row],
        out_specs=[row, row],
        out_shape=[jax.ShapeDtypeStruct((m, D_MODEL), F32)] * 2,
        compiler_params=_params(1),
        name="short_conv_tok",
    )(x, g_pre, win_bf16, conv_w, wout_bf16, g_post, inj1, inj2)
    return out, u.reshape(nb, seq, D_MODEL)[:, seq - (CONV_WIDTH - 1):]


def kernel(x_prompt, x_sample, cache_k, cache_v, state_conv, page_table, rel_bias, w_qkv, w_o, lambda_q1, lambda_k1, lambda_q2, lambda_k2, subln_g, w_conv_in, conv_w, w_conv_out, norm_mix_pre, norm_mix_post, norm_ffn_pre, norm_ffn_post, w_ffn_gu, w_ffn_down):
    b, s, d = x_prompt.shape
    nb, dec_seq, _ = x_sample.shape
    assert d == D_MODEL and page_table.shape[1] * PAGE_SIZE >= FAR_DISTANCE
    mp, ms = b * s, nb * dec_seq
    tm_p = TOKEN_TILE

    wkv = w_qkv[:, ATTN_WIDTH:].astype(BF16)
    wqt = w_qkv[:, :ATTN_WIDTH].T.astype(BF16)
    wvt = w_qkv[:, 2 * ATTN_WIDTH:].T.astype(BF16)
    wo = w_o.astype(BF16)
    win = w_conv_in.astype(BF16)
    wout = w_conv_out.astype(BF16)
    wgu = w_ffn_gu.astype(BF16)
    wd = w_ffn_down.astype(BF16)
    gain = lambda g, i: g[i].reshape(1, D_MODEL)
    lam_vecs = jnp.stack([lambda_q1, lambda_k1, lambda_q2, lambda_k2])
    sub_g = subln_g.reshape(1, V_DIM)
    lam0 = _lambda_init(0)

    xp = x_prompt.reshape(mp, d)
    xs = x_sample.reshape(ms, d)

    g_pre = gain(norm_mix_pre, 0)
    kf_p, vf_p, kb_p, qt_p, vt_p = _qkv_call(xp, g_pre, wkv, wqt, wvt, tm_p, s)
    kf_s, vf_s, kb_s, qt_s, _ = _qkv_call(xs, g_pre, wkv, wqt, wvt, ms, ms)

    on_p = _prompt_attn_call(qt_p, kb_p.reshape(b, s, ATTN_WIDTH), vt_p, rel_bias, lam_vecs,
                             subln_g.reshape(V_DIM, 1), lam0, ATTN_QUERY_TILE, ATTN_KEY_TILE)

    q4 = qt_s.reshape(N_HEADS, QK_DIM, nb, dec_seq).transpose(2, 0, 3, 1)
    q_rows = jnp.stack([q4, q4], axis=2).reshape(nb, N_HEADS * 2 * dec_seq, QK_DIM)
    pad_tok = PAGE_SIZE // N_HEADS - dec_seq
    flat_new = lambda a: jnp.pad(a.reshape(nb, dec_seq, N_HEADS, QK_DIM),
                                 ((0, 0), (0, pad_tok), (0, 0), (0, 0))).reshape(nb, PAGE_SIZE, QK_DIM)
    on_rows = _sample_attn_call(q_rows, flat_new(kb_s), flat_new(vf_s.astype(BF16)), cache_k, cache_v,
                                page_table, rel_bias, lam_vecs, sub_g, lam0, dec_seq)
    on_s = on_rows.reshape(nb, N_HEADS, 2, dec_seq, V_DIM)[:, :, 0].transpose(0, 2, 1, 3)
    on_s = on_s.reshape(ms, ATTN_WIDTH).astype(BF16)

    g_post = gain(norm_mix_post, 0)
    xp = _proj_call(xp, on_p.reshape(mp, ATTN_WIDTH), wo, g_post, tm_p)
    xs = _proj_call(xs, on_s, wo, g_post, ms)
    ffn = lambda x, i, tm: _ffn_call(x, gain(norm_ffn_pre, i), wgu[i], wd[i], gain(norm_ffn_post, i), tm)
    xp = ffn(xp, 0, tm_p)
    xs = ffn(xs, 0, ms)

    g_pre, g_post = gain(norm_mix_pre, 1), gain(norm_mix_post, 1)
    zero_state = jnp.zeros((b, CONV_WIDTH - 1, d), F32)
    xp, conv_p = _conv_call(xp, g_pre, win, conv_w, wout, g_post, zero_state, tm_p, s)
    xs, conv_s = _conv_call(xs, g_pre, win, conv_w, wout, g_post, state_conv, ms, dec_seq)
    xp = ffn(xp, 1, tm_p)
    xs = ffn(xs, 1, ms)

    return (xp.reshape(b, s, d), xs.reshape(nb, dec_seq, d),
            kf_p.reshape(b, s, N_HEADS, QK_DIM), vf_p.reshape(b, s, N_HEADS, V_DIM),
            kf_s.reshape(nb, dec_seq, N_HEADS, QK_DIM), vf_s.reshape(nb, dec_seq, N_HEADS, V_DIM),
            conv_p, conv_s)
```

```python
import functools
import math

import jax
import jax.numpy as jnp
from jax import lax
from jax.experimental import pallas as pl
from jax.experimental.pallas import tpu as pltpu

D_MODEL = 1024
N_HEADS = 8
HEAD_DIM = 64
QK_DIM = 2 * HEAD_DIM
V_DIM = 2 * HEAD_DIM
ATTN_WIDTH = N_HEADS * V_DIM
N_BUCKETS = 32
MAX_DISTANCE = 128
CONV_WIDTH = 3
PAGE_SIZE = 128
EPS = 1e-6
NEG_INF = -1e30
SCALE = HEAD_DIM ** -0.5
LOG2E = math.log2(math.e)

V7X_LANES = 128
V7X_BF16_SUBLANES = 16
V_EXT = V_DIM + V7X_BF16_SUBLANES
V7X_SUBLANES = 8
VMEM_LIMIT_BYTES = 56 * 1024 * 1024
SAMPLE_PAGES_PER_STEP = 8
TOKEN_TILE = 512
ATTN_QUERY_TILE = 512
ATTN_KEY_TILE = 256

F32 = jnp.float32
BF16 = jnp.bfloat16


def _lambda_init(layer):
    return 0.8 - 0.6 * math.exp(-0.3 * layer)


def _bucket_uppers():
    max_exact = N_BUCKETS // 2
    n_far = None
    buckets = []
    for n in range(0, 4 * MAX_DISTANCE):
        if n < max_exact:
            b = n
        else:
            r = math.log(n / max_exact) / math.log(MAX_DISTANCE / max_exact) * (N_BUCKETS - max_exact)
            frac = r - math.floor(r)
            assert min(frac, 1 - frac) > 1e-4 or n == max_exact or n >= MAX_DISTANCE, n
            b = min(max_exact + int(r + (1e-6 if n == max_exact else 0.0)), N_BUCKETS - 1)
            if n >= MAX_DISTANCE:
                b = N_BUCKETS - 1
        buckets.append(b)
    uppers = []
    for b in range(N_BUCKETS - 1):
        uppers.append(next(n for n, bb in enumerate(buckets) if bb > b))
    n_far = uppers[-1]
    return tuple(uppers), n_far


BUCKET_UPPERS, FAR_DISTANCE = _bucket_uppers()


def _shifted_bias(n, relb_ref, head):
    far = relb_ref[N_BUCKETS - 1, head]
    val = jnp.zeros(n.shape, F32)
    for b in range(N_BUCKETS - 2, -1, -1):
        val = jnp.where(n < BUCKET_UPPERS[b], (relb_ref[b, head] - far) * LOG2E, val)
    return val


def _diff_lambda(lam_ref, lam_init):
    lv = lam_ref[...]
    a = jnp.sum(lv[0:1] * lv[1:2], axis=1, keepdims=True)
    b = jnp.sum(lv[2:3] * lv[3:4], axis=1, keepdims=True)
    return jnp.exp(a) - jnp.exp(b) + lam_init


def _rms(x, g):
    return x * lax.rsqrt(jnp.mean(x * x, axis=-1, keepdims=True) + EPS) * g


def _head_norm(o, g, lam_init):
    return o * lax.rsqrt(jnp.mean(o * o, axis=-1, keepdims=True) + EPS) * g * (1.0 - lam_init)


def _const_spec(shape):
    return pl.BlockSpec(shape, lambda *_: (0,) * len(shape), pipeline_mode=pl.Buffered(1))


def _params(n_grid):
    return pltpu.CompilerParams(dimension_semantics=("arbitrary",) * n_grid,
                                vmem_limit_bytes=VMEM_LIMIT_BYTES)


def _qkv_kernel(x_ref, g_ref, w_ref, kf_ref, vf_ref, kb_ref, qt_ref, vt_ref):
    hb = _rms(x_ref[...], g_ref[...]).astype(BF16)
    w = ATTN_WIDTH
    tm = hb.shape[0]
    q = jnp.dot(hb, w_ref[:, 0:w], preferred_element_type=F32)
    qt_ref[...] = (q * (SCALE * LOG2E)).T.astype(BF16).reshape(N_HEADS, QK_DIM, tm)
    k = jnp.dot(hb, w_ref[:, w:2 * w], preferred_element_type=F32)
    kf_ref[...] = k
    kb_ref[...] = k.astype(BF16)
    v = jnp.dot(hb, w_ref[:, 2 * w:3 * w], preferred_element_type=F32)
    vf_ref[...] = v
    vt_ref[:, 0:V_DIM, :] = v.T.astype(BF16).reshape(N_HEADS, V_DIM, tm)
    vt_ref[:, V_DIM:V_EXT, :] = jnp.ones((N_HEADS, V_EXT - V_DIM, tm), BF16)


def _qkv_call(x, g, w_bf16, tm, seq):
    m = x.shape[0]
    tiles_per_seq = seq // tm
    row = lambda width: pl.BlockSpec((tm, width), lambda i: (i, 0))
    fmaj = lambda rows: pl.BlockSpec((None, N_HEADS, rows, tm),
                                     lambda i: (i // tiles_per_seq, 0, 0, i % tiles_per_seq))
    return pl.pallas_call(
        _qkv_kernel,
        grid=(m // tm,),
        in_specs=[row(D_MODEL), _const_spec((1, D_MODEL)), _const_spec(w_bf16.shape)],
        out_specs=[row(ATTN_WIDTH)] * 3 + [fmaj(QK_DIM), fmaj(V_EXT)],
        out_shape=[jax.ShapeDtypeStruct((m, ATTN_WIDTH), F32)] * 2
        + [jax.ShapeDtypeStruct((m, ATTN_WIDTH), BF16)]
        + [jax.ShapeDtypeStruct((m // seq, N_HEADS, QK_DIM, seq), BF16),
           jax.ShapeDtypeStruct((m // seq, N_HEADS, V_EXT, seq), BF16)],
        compiler_params=_params(1),
        name="qkv_proj",
    )(x, g, w_bf16)


def _prompt_attn_kernel(relb_ref, lam_ref, g_ref, qt_ref, k_ref, vt_ref, o_ref,
                        q2_ref, m_ref, acc_ref, bias_ref, s0_ref, s1_ref, p0_ref, p1_ref,
                        a0_ref, a1_ref, t0_ref, t1_ref, *, tq, tk, lam_init):
    head = pl.program_id(0)
    qi = pl.program_id(1)
    chains = range(qt_ref.shape[0])
    ratio = tq // tk
    n_band = ratio + 1
    last = ratio * (qi + 1) - 1

    @pl.when(qi == 0)
    def _build_bias():
        j = lax.broadcasted_iota(jnp.int32, (tk, tq), 0)
        i = lax.broadcasted_iota(jnp.int32, (tk, tq), 1)
        for d in range(n_band):
            n = i - j + tk * (d + 1 - ratio)
            bias_ref[d] = jnp.where(n >= 0, _shifted_bias(n, relb_ref, head), NEG_INF)
        bias_ref[n_band] = jnp.zeros((tk, tq), F32)

    feat = lax.broadcasted_iota(jnp.int32, (QK_DIM, tq), 0)
    for c in chains:
        qt = qt_ref[c]
        zero = jnp.zeros_like(qt)
        q2_ref[c, :, 0:tq] = jnp.where(feat < HEAD_DIM, qt, zero)
        q2_ref[c, :, tq:2 * tq] = jnp.where(feat >= HEAD_DIM, qt, zero)
    m_ref[...] = jnp.full(m_ref.shape, NEG_INF, F32)
    acc_ref[...] = jnp.zeros(acc_ref.shape, F32)
    for p_buf, a_buf in ((p0_ref, a0_ref), (p1_ref, a1_ref)):
        p_buf[...] = jnp.zeros(p_buf.shape, BF16)
        a_buf[...] = jnp.ones(a_buf.shape, F32)

    @pl.when((head == 0) & (qi == 0))
    def _first_touch():
        for s_buf, t_buf in ((s0_ref, t0_ref), (s1_ref, t1_ref)):
            s_buf[...] = jnp.zeros(s_buf.shape, F32)
            t_buf[...] = jnp.zeros(t_buf.shape, F32)

    def time_step(tau, parity, *, scores=True, softmax=True, with_bias=False):
        s_buf, t_buf = (s0_ref, t0_ref) if parity == 0 else (s1_ref, t1_ref)
        p_buf, a_buf = (p0_ref, a0_ref) if parity == 0 else (p1_ref, a1_ref)
        jv = pl.multiple_of(jnp.clip(tau - 4, 0, last) * tk, tk)
        valid = (tau >= 2) & (tau - 2 <= last)
        jq = jnp.minimum(tau, last)
        start = pl.multiple_of(jq * tk, tk)
        if scores and with_bias:
            b = bias_ref[jnp.minimum(last - jq, n_band)]
            b2 = jnp.concatenate([b, b], axis=1)

        def values_stage(c):
            pv = jnp.dot(vt_ref[c, :, pl.ds(jv, tk)], p_buf[c], preferred_element_type=F32)
            acc_ref[c] = a_buf[c] * acc_ref[c] + pv

        def softmax_stage(c):
            m_prev = m_ref[c]
            m_new = jnp.where(valid, jnp.maximum(m_prev, t_buf[c]), m_prev)
            a_buf[c] = jnp.exp2(m_prev - m_new)
            p_buf[c] = jnp.exp2((s_buf[c] - jnp.where(valid, m_new, -NEG_INF)).astype(BF16))
            m_ref[c] = m_new

        def scores_stage(c):
            s_new = jnp.dot(k_ref[c, pl.ds(start, tk), :], q2_ref[c],
                            preferred_element_type=F32)
            if with_bias:
                s_new = s_new + b2
            s_buf[c] = s_new
            t_buf[c] = jnp.max(s_new, axis=0, keepdims=True)

        for c in chains:
            values_stage(c)
            if softmax:
                softmax_stage(c)
            if scores:
                scores_stage(c)

    n_far = jnp.maximum(last + 1 - n_band, 0)
    n_quads = n_far // 4
    n_pairs = n_far // 2

    def far_quad(u, carry):
        for d in range(4):
            time_step(4 * u + d, d % 2)
        return carry

    def far_pair(u, carry):
        time_step(2 * u, 0)
        time_step(2 * u + 1, 1)
        return carry

    lax.fori_loop(0, n_quads, far_quad, 0)
    lax.fori_loop(2 * n_quads, n_pairs, far_pair, 0)
    tau0 = 2 * n_pairs
    n_tail = n_band + 1
    for d in range(n_tail + 4):
        time_step(tau0 + d, d % 2, scores=d < n_tail, softmax=d < n_tail + 2, with_bias=True)

    lam = _diff_lambda(lam_ref, lam_init)
    gain = g_ref[...] * (1.0 - lam_init)
    for c in chains:
        inv_l = 1.0 / acc_ref[c, V_DIM:V_DIM + 1, :]
        o1 = acc_ref[c, 0:V_DIM, 0:tq] * inv_l[:, 0:tq]
        o2 = acc_ref[c, 0:V_DIM, tq:2 * tq] * inv_l[:, tq:2 * tq]
        ot = o1 - lam * o2
        ms = jnp.mean(ot * ot, axis=0, keepdims=True)
        o_ref[c] = (ot * lax.rsqrt(ms + EPS) * gain).T.astype(BF16)


def _prompt_attn_call(qt, kb, vt, rel_bias, lam_vecs, subln_g_col, lam_init, tq, tk):
    b, _, _, s = qt.shape
    assert s % tq == 0 and tq % tk == 0 and tk >= FAR_DISTANCE
    out_tile = pl.BlockSpec((b, tq, V_DIM), lambda h, qi: (0, qi, h))
    q_tile = pl.BlockSpec((b, None, QK_DIM, tq), lambda h, qi: (0, h, 0, qi))
    k_seq = pl.BlockSpec((b, s, QK_DIM), lambda h, qi: (0, 0, h))
    v_seq = pl.BlockSpec((b, None, V_EXT, s), lambda h, qi: (0, h, 0, 0))
    stat = pltpu.VMEM((b, 1, 2 * tq), F32)
    return pl.pallas_call(
        functools.partial(_prompt_attn_kernel, tq=tq, tk=tk, lam_init=lam_init),
        grid=(N_HEADS, s // tq),
        in_specs=[pl.BlockSpec(memory_space=pltpu.SMEM), _const_spec((4, HEAD_DIM)),
                  _const_spec((V_DIM, 1)), q_tile, k_seq, v_seq],
        out_specs=out_tile,
        out_shape=jax.ShapeDtypeStruct((b, s, ATTN_WIDTH), BF16),
        scratch_shapes=[pltpu.VMEM((b, QK_DIM, 2 * tq), BF16), stat,
                        pltpu.VMEM((b, V_EXT, 2 * tq), F32),
                        pltpu.VMEM((tq // tk + 2, tk, tq), F32),
                        pltpu.VMEM((b, tk, 2 * tq), F32), pltpu.VMEM((b, tk, 2 * tq), F32),
                        pltpu.VMEM((b, tk, 2 * tq), BF16), pltpu.VMEM((b, tk, 2 * tq), BF16),
                        stat, stat, stat, stat],
        compiler_params=_params(2),
        name="prompt_attn",
    )(rel_bias, lam_vecs, subln_g_col, qt, kb, vt)


def _sample_attn_kernel(pt_ref, relb_ref, lam_ref, g_ref, q_ref, kn_ref, vn_ref, *rest,
                        dec_seq, n_steps, group, lam_init):
    del pt_ref
    kc_refs, vc_refs = rest[:group], rest[group:2 * group]
    o_ref, m_ref, l_ref, acc_ref, qm_ref, mask_ref, lastb_ref, newb_ref = rest[2 * group:]
    step = pl.program_id(1)
    rows = N_HEADS * 2 * dec_seq
    per_head = 2 * dec_seq
    assert per_head == V7X_SUBLANES
    flat = PAGE_SIZE * N_HEADS
    nt = (((1,), (1,)), ((), ()))

    def grids(width):
        r = lax.broadcasted_iota(jnp.int32, (rows, width), 0)
        c = lax.broadcasted_iota(jnp.int32, (rows, width), 1)
        same_head = (c % N_HEADS) == (r // per_head)
        return r % dec_seq, c // N_HEADS, same_head

    def head_bias(dist):
        parts = [_shifted_bias(dist[h * per_head:(h + 1) * per_head], relb_ref, h)
                 for h in range(N_HEADS)]
        return jnp.concatenate(parts, axis=0)

    @pl.when((pl.program_id(0) == 0) & (step == 0))
    def _tables():
        tq, tk, ok = grids(flat)
        mask_ref[...] = jnp.where(ok, 0.0, NEG_INF)
        lastb_ref[...] = jnp.where(ok, head_bias(PAGE_SIZE + tq - tk), NEG_INF)
        tqn, tkn, okn = grids(kn_ref.shape[0])
        newb_ref[...] = jnp.where(okn & (tkn <= tqn), head_bias(tqn - tkn), NEG_INF)

    @pl.when(step == 0)
    def _init():
        m_ref[...] = jnp.full(m_ref.shape, NEG_INF, F32)
        l_ref[...] = jnp.zeros(l_ref.shape, F32)
        acc_ref[...] = jnp.zeros(acc_ref.shape, F32)
        q = q_ref[...]
        row = lax.broadcasted_iota(jnp.int32, (rows, QK_DIM), 0)
        lane = lax.broadcasted_iota(jnp.int32, (rows, QK_DIM), 1)
        first_comp = (row // dec_seq) % 2 == 0
        qm_ref[...] = jnp.where((lane < HEAD_DIM) == first_comp, q, jnp.zeros_like(q))

    def update(s_list, v_list):
        m_prev = m_ref[...]
        m_tile = functools.reduce(jnp.maximum, [jnp.max(s, axis=1, keepdims=True) for s in s_list])
        m_new = jnp.maximum(m_prev, m_tile)
        alpha = jnp.exp2(m_prev - m_new)
        p_list = [jnp.exp2(s - m_new[:, 0:1]) for s in s_list]
        l_ref[...] = alpha * l_ref[...] + sum(jnp.sum(p, axis=1, keepdims=True) for p in p_list)
        pv = sum(jnp.dot(p.astype(BF16), v(), preferred_element_type=F32)
                 for p, v in zip(p_list, v_list))
        acc_ref[...] = alpha * acc_ref[...] + pv
        m_ref[...] = m_new

    qm = qm_ref[...]

    def scores(keys, bias_ref):
        return lax.dot_general(qm, keys, nt, preferred_element_type=F32) + bias_ref[...]

    def page_update(lo, hi, last_bias_ref):
        biases = [mask_ref] * (hi - lo - 1) + [last_bias_ref]
        update([scores(kc[...].reshape(flat, QK_DIM).astype(BF16), b)
                for kc, b in zip(kc_refs[lo:hi], biases)],
               [lambda vc=vc: vc[...].reshape(flat, V_DIM).astype(BF16) for vc in vc_refs[lo:hi]])

    @pl.when(step < n_steps - 1)
    def _far_pages():
        page_update(0, group, mask_ref)

    @pl.when(step == n_steps - 1)
    def _last_pages():
        page_update(0, group, lastb_ref)
        update([scores(kn_ref[...], newb_ref)], [lambda: vn_ref[...]])
        o = acc_ref[...] / l_ref[...]
        o2 = pltpu.roll(o, rows - dec_seq, axis=0)
        od = o - _diff_lambda(lam_ref, lam_init) * o2
        o_ref[...] = _head_norm(od, g_ref[...], lam_init)


def _sample_attn_call(q_rows, k_new, v_new, cache_k, cache_v, page_table, rel_bias, lam_vecs,
                      subln_g, lam_init, dec_seq):
    nb, rows, _ = q_rows.shape
    n_pages = page_table.shape[1]
    group = SAMPLE_PAGES_PER_STEP
    assert PAGE_SIZE >= FAR_DISTANCE + dec_seq and n_pages % group == 0
    n_steps = n_pages // group
    flat = PAGE_SIZE * N_HEADS
    per_b = lambda shape: pl.BlockSpec((None,) + shape, lambda b, p, pt: (b, 0, 0))
    paged = lambda g: pl.BlockSpec((None, PAGE_SIZE, N_HEADS, QK_DIM),
                                   lambda b, p, pt: (pt[b, p * group + g], 0, 0, 0))
    const = lambda shape: pl.BlockSpec(shape, lambda b, p, pt: (0,) * len(shape))
    grid_spec = pltpu.PrefetchScalarGridSpec(
        num_scalar_prefetch=1,
        grid=(nb, n_steps),
        in_specs=[pl.BlockSpec(memory_space=pltpu.SMEM), const((4, HEAD_DIM)), const((1, V_DIM)),
                  per_b((rows, QK_DIM)), per_b(k_new.shape[1:]), per_b(v_new.shape[1:])]
        + [paged(g) for g in range(group)] * 2,
        out_specs=per_b((rows, V_DIM)),
        scratch_shapes=[pltpu.VMEM((rows, V7X_LANES), F32),
                        pltpu.VMEM((rows, V7X_LANES), F32),
                        pltpu.VMEM((rows, V_DIM), F32),
                        pltpu.VMEM((rows, QK_DIM), BF16),
                        pltpu.VMEM((rows, flat), F32),
                        pltpu.VMEM((rows, flat), F32),
                        pltpu.VMEM((rows, k_new.shape[1]), F32)],
    )
    return pl.pallas_call(
        functools.partial(_sample_attn_kernel, dec_seq=dec_seq, n_steps=n_steps, group=group,
                          lam_init=lam_init),
        grid_spec=grid_spec,
        out_shape=jax.ShapeDtypeStruct((nb, rows, V_DIM), F32),
        compiler_params=_params(2),
        name="sample_attn",
    )(page_table, rel_bias, lam_vecs, subln_g, q_rows, k_new, v_new,
      *([cache_k] * group), *([cache_v] * group))


def _ffn_kernel(*refs, d_ff, with_proj):
    if with_proj:
        x_ref, a_ref, wo_ref, gmix_ref, gpre_ref, wgu_ref, wd_ref, gpost_ref, o_ref = refs
        x = x_ref[...] + _rms(jnp.dot(a_ref[...], wo_ref[...], preferred_element_type=F32), gmix_ref[...])
    else:
        x_ref, gpre_ref, wgu_ref, wd_ref, gpost_ref, o_ref = refs
        x = x_ref[...]
    hb = _rms(x, gpre_ref[...]).astype(BF16)
    g = jnp.dot(hb, wgu_ref[:, 0:d_ff], preferred_element_type=F32)
    u = jnp.dot(hb, wgu_ref[:, d_ff:2 * d_ff], preferred_element_type=F32)
    a = (g * (1.0 / (1.0 + jnp.exp(-g))) * u).astype(BF16)
    y = jnp.dot(a, wd_ref[...], preferred_element_type=F32)
    o_ref[...] = x + _rms(y, gpost_ref[...])


def _ffn_call(x, g_pre, wgu_bf16, wd_bf16, g_post, tm, proj=None):
    m = x.shape[0]
    d_ff = wd_bf16.shape[0]
    row = lambda width: pl.BlockSpec((tm, width), lambda i: (i, 0))
    ffn_specs = [_const_spec((1, D_MODEL)), _const_spec(wgu_bf16.shape),
                 _const_spec(wd_bf16.shape), _const_spec((1, D_MODEL))]
    ffn_args = (g_pre, wgu_bf16, wd_bf16, g_post)
    if proj is None:
        specs, args = [row(D_MODEL)] + ffn_specs, (x,) + ffn_args
    else:
        a_bf16, wo_bf16, g_mix = proj
        specs = [row(D_MODEL), row(a_bf16.shape[1]), _const_spec(wo_bf16.shape),
                 _const_spec((1, D_MODEL))] + ffn_specs
        args = (x, a_bf16, wo_bf16, g_mix) + ffn_args
    return pl.pallas_call(
        functools.partial(_ffn_kernel, d_ff=d_ff, with_proj=proj is not None),
        grid=(m // tm,),
        in_specs=specs,
        out_specs=row(D_MODEL),
        out_shape=jax.ShapeDtypeStruct((m, D_MODEL), F32),
        compiler_params=_params(1),
        name="swiglu_ffn",
    )(*args)


def _conv_kernel(*refs, tm, seq):
    carry_mode = seq >= tm
    if carry_mode:
        (x_ref, gpre_ref, win_ref, cw_ref, wout_ref, gpost_ref, st_ref,
         o_ref, tail_ref, carry_ref) = refs
    else:
        (x_ref, gpre_ref, win_ref, cw_ref, wout_ref, gpost_ref, inj1_ref, inj2_ref,
         o_ref, u_ref) = refs
    d = D_MODEL
    x = x_ref[...]
    hb = _rms(x, gpre_ref[...]).astype(BF16)
    gc = jnp.dot(hb, win_ref[:, d:2 * d], preferred_element_type=F32)
    xt = jnp.dot(hb, win_ref[:, 2 * d:3 * d], preferred_element_type=F32)
    u = gc * xt
    row = lax.broadcasted_iota(jnp.int32, (tm, d), 0)
    r1 = pltpu.roll(u, 1, axis=0)
    r2 = pltpu.roll(u, 2, axis=0)
    if carry_mode:
        i = pl.program_id(0)
        tiles_per_seq = seq // tm

        @pl.when(i % tiles_per_seq == 0)
        def _seq_start():
            carry_ref[...] = st_ref[...]

        c = carry_ref[...]
        c1 = c[V7X_SUBLANES - 1:V7X_SUBLANES]
        c2 = c[V7X_SUBLANES - 2:V7X_SUBLANES - 1]
        prev1 = jnp.where(row == 0, c1, r1)
        prev2 = jnp.where(row == 0, c2, jnp.where(row == 1, c1, r2))
        tail = u[tm - V7X_SUBLANES:tm]
        carry_ref[...] = tail
        tail_ref[...] = tail
    else:
        t = row % seq
        prev1 = jnp.where(t >= 1, r1, inj1_ref[...])
        prev2 = jnp.where(t >= 2, r2, inj2_ref[...])
        u_ref[...] = u
    cw = cw_ref[...]
    y = cw[0:1] * prev2 + cw[1:2] * prev1 + cw[2:3] * u
    gb = jnp.dot(hb, win_ref[:, 0:d], preferred_element_type=F32)
    z = jnp.dot((gb * y).astype(BF16), wout_ref[...], preferred_element_type=F32)
    o_ref[...] = x + _rms(z, gpost_ref[...])


def _conv_call(x, g_pre, win_bf16, conv_w, wout_bf16, g_post, state, tm, seq):
    m = x.shape[0]
    nb = m // seq
    row = pl.BlockSpec((tm, D_MODEL), lambda i: (i, 0))
    common = [row, _const_spec((1, D_MODEL)), _const_spec(win_bf16.shape),
              _const_spec((CONV_WIDTH, D_MODEL)), _const_spec(wout_bf16.shape),
              _const_spec((1, D_MODEL))]
    kern = functools.partial(_conv_kernel, tm=tm, seq=seq)
    if seq >= tm:
        assert seq % tm == 0
        n_tiles = m // tm
        tiles_per_seq = seq // tm
        pad = jnp.zeros((nb, V7X_SUBLANES - (CONV_WIDTH - 1), D_MODEL), F32)
        st = jnp.concatenate([pad, state], axis=1)
        st_spec = pl.BlockSpec((None, V7X_SUBLANES, D_MODEL), lambda i: (i // tiles_per_seq, 0, 0))
        tail_spec = pl.BlockSpec((None, V7X_SUBLANES, D_MODEL), lambda i: (i, 0, 0))
        out, tails = pl.pallas_call(
            kern, grid=(n_tiles,),
            in_specs=common + [st_spec],
            out_specs=[row, tail_spec],
            out_shape=[jax.ShapeDtypeStruct((m, D_MODEL), F32),
                       jax.ShapeDtypeStruct((n_tiles, V7X_SUBLANES, D_MODEL), F32)],
            scratch_shapes=[pltpu.VMEM((V7X_SUBLANES, D_MODEL), F32)],
            compiler_params=_params(1),
            name="short_conv_seq",
        )(x, g_pre, win_bf16, conv_w, wout_bf16, g_post, st)
        last = tails.reshape(nb, tiles_per_seq, V7X_SUBLANES, D_MODEL)[:, -1]
        return out, last[:, V7X_SUBLANES - (CONV_WIDTH - 1):]
    assert tm % seq == 0 and m == tm and seq >= CONV_WIDTH - 1
    zeros = lambda n: jnp.zeros((nb, n, D_MODEL), F32)
    inj1 = jnp.concatenate([state[:, 1:2], zeros(seq - 1)], axis=1).reshape(m, D_MODEL)
    inj2 = jnp.concatenate([state[:, 0:2], zeros(seq - 2)], axis=1).reshape(m, D_MODEL)
    out, u = pl.pallas_call(
        kern, grid=(1,),
        in_specs=common + [row, row],
        out_specs=[row, row],
        out_shape=[jax.ShapeDtypeStruct((m, D_MODEL), F32)] * 2,
        compiler_params=_params(1),
        name="short_conv_tok",
    )(x, g_pre, win_bf16, conv_w, wout_bf16, g_post, inj1, inj2)
    return out, u.reshape(nb, seq, D_MODEL)[:, seq - (CONV_WIDTH - 1):]


def kernel(x_prompt, x_sample, cache_k, cache_v, state_conv, page_table, rel_bias, w_qkv, w_o, lambda_q1, lambda_k1, lambda_q2, lambda_k2, subln_g, w_conv_in, conv_w, w_conv_out, norm_mix_pre, norm_mix_post, norm_ffn_pre, norm_ffn_post, w_ffn_gu, w_ffn_down):
    b, s, d = x_prompt.shape
    nb, dec_seq, _ = x_sample.shape
    assert d == D_MODEL and page_table.shape[1] * PAGE_SIZE >= FAR_DISTANCE
    mp, ms = b * s, nb * dec_seq
    tm_p = TOKEN_TILE

    wqkv = w_qkv.astype(BF16)
    wo = w_o.astype(BF16)
    win = w_conv_in.astype(BF16)
    wout = w_conv_out.astype(BF16)
    wgu = w_ffn_gu.astype(BF16)
    wd = w_ffn_down.astype(BF16)
    gain = lambda g, i: g[i].reshape(1, D_MODEL)
    lam_vecs = jnp.stack([lambda_q1, lambda_k1, lambda_q2, lambda_k2])
    sub_g = subln_g.reshape(1, V_DIM)
    lam0 = _lambda_init(0)

    xp = x_prompt.reshape(mp, d)
    xs = x_sample.reshape(ms, d)

    g_pre = gain(norm_mix_pre, 0)
    kf_p, vf_p, kb_p, qt_p, vt_p = _qkv_call(xp, g_pre, wqkv, tm_p, s)
    kf_s, vf_s, kb_s, qt_s, _ = _qkv_call(xs, g_pre, wqkv, ms, ms)

    on_p = _prompt_attn_call(qt_p, kb_p.reshape(b, s, ATTN_WIDTH), vt_p, rel_bias, lam_vecs,
                             subln_g.reshape(V_DIM, 1), lam0, ATTN_QUERY_TILE, ATTN_KEY_TILE)

    q4 = qt_s.reshape(N_HEADS, QK_DIM, nb, dec_seq).transpose(2, 0, 3, 1)
    q_rows = jnp.stack([q4, q4], axis=2).reshape(nb, N_HEADS * 2 * dec_seq, QK_DIM)
    pad_tok = PAGE_SIZE // N_HEADS - dec_seq
    flat_new = lambda a: jnp.pad(a.reshape(nb, dec_seq, N_HEADS, QK_DIM),
                                 ((0, 0), (0, pad_tok), (0, 0), (0, 0))).reshape(nb, PAGE_SIZE, QK_DIM)
    on_rows = _sample_attn_call(q_rows, flat_new(kb_s), flat_new(vf_s.astype(BF16)), cache_k, cache_v,
                                page_table, rel_bias, lam_vecs, sub_g, lam0, dec_seq)
    on_s = on_rows.reshape(nb, N_HEADS, 2, dec_seq, V_DIM)[:, :, 0].transpose(0, 2, 1, 3)
    on_s = on_s.reshape(ms, ATTN_WIDTH).astype(BF16)

    g_post = gain(norm_mix_post, 0)
    ffn = lambda x, i, tm, proj=None: _ffn_call(x, gain(norm_ffn_pre, i), wgu[i], wd[i],
                                                gain(norm_ffn_post, i), tm, proj)
    xp = ffn(xp, 0, tm_p, (on_p.reshape(mp, ATTN_WIDTH), wo, g_post))
    xs = ffn(xs, 0, ms, (on_s, wo, g_post))

    g_pre, g_post = gain(norm_mix_pre, 1), gain(norm_mix_post, 1)
    zero_state = jnp.zeros((b, CONV_WIDTH - 1, d), F32)
    xp, conv_p = _conv_call(xp, g_pre, win, conv_w, wout, g_post, zero_state, tm_p, s)
    xs, conv_s = _conv_call(xs, g_pre, win, conv_w, wout, g_post, state_conv, ms, dec_seq)
    xp = ffn(xp, 1, tm_p)
    xs = ffn(xs, 1, ms)

    return (xp.reshape(b, s, d), xs.reshape(nb, dec_seq, d),
            kf_p.reshape(b, s, N_HEADS, QK_DIM), vf_p.reshape(b, s, N_HEADS, V_DIM),
            kf_s.reshape(nb, dec_seq, N_HEADS, QK_DIM), vf_s.reshape(nb, dec_seq, N_HEADS, V_DIM),
            conv_p, conv_s)
```

```python
import functools
import math

import jax
import jax.numpy as jnp
from jax import lax
from jax.experimental import pallas as pl
from jax.experimental.pallas import tpu as pltpu

D_MODEL = 1024
N_HEADS = 8
HEAD_DIM = 64
QK_DIM = 2 * HEAD_DIM
V_DIM = 2 * HEAD_DIM
ATTN_WIDTH = N_HEADS * V_DIM
N_BUCKETS = 32
MAX_DISTANCE = 128
CONV_WIDTH = 3
PAGE_SIZE = 128
EPS = 1e-6
NEG_INF = -1e30
SCALE = HEAD_DIM ** -0.5
LOG2E = math.log2(math.e)

V7X_LANES = 128
V7X_BF16_SUBLANES = 16
V_EXT = V_DIM + V7X_BF16_SUBLANES
V7X_SUBLANES = 8
VMEM_LIMIT_BYTES = 56 * 1024 * 1024
SAMPLE_PAGES_PER_STEP = 8
TOKEN_TILE = 512
ATTN_QUERY_TILE = 512
ATTN_KEY_TILE = 256
FAR_TRIP_PAIRS = (2, 1)
F32 = jnp.float32
BF16 = jnp.bfloat16


def _lambda_init(layer):
    return 0.8 - 0.6 * math.exp(-0.3 * layer)


def _bucket_uppers():
    max_exact = N_BUCKETS // 2
    n_far = None
    buckets = []
    for n in range(0, 4 * MAX_DISTANCE):
        if n < max_exact:
            b = n
        else:
            r = math.log(n / max_exact) / math.log(MAX_DISTANCE / max_exact) * (N_BUCKETS - max_exact)
            frac = r - math.floor(r)
            assert min(frac, 1 - frac) > 1e-4 or n == max_exact or n >= MAX_DISTANCE, n
            b = min(max_exact + int(r + (1e-6 if n == max_exact else 0.0)), N_BUCKETS - 1)
            if n >= MAX_DISTANCE:
                b = N_BUCKETS - 1
        buckets.append(b)
    uppers = []
    for b in range(N_BUCKETS - 1):
        uppers.append(next(n for n, bb in enumerate(buckets) if bb > b))
    n_far = uppers[-1]
    return tuple(uppers), n_far


BUCKET_UPPERS, FAR_DISTANCE = _bucket_uppers()


def _shifted_bias(n, relb_ref, head):
    far = relb_ref[N_BUCKETS - 1, head]
    val = jnp.zeros(n.shape, F32)
    for b in range(N_BUCKETS - 2, -1, -1):
        val = jnp.where(n < BUCKET_UPPERS[b], (relb_ref[b, head] - far) * LOG2E, val)
    return val


def _diff_lambda(lam_ref, lam_init):
    lv = lam_ref[...]
    a = jnp.sum(lv[0:1] * lv[1:2], axis=1, keepdims=True)
    b = jnp.sum(lv[2:3] * lv[3:4], axis=1, keepdims=True)
    return jnp.exp(a) - jnp.exp(b) + lam_init


def _rms(x, g):
    return x * lax.rsqrt(jnp.mean(x * x, axis=-1, keepdims=True) + EPS) * g


def _head_norm(o, g, lam_init):
    return o * lax.rsqrt(jnp.mean(o * o, axis=-1, keepdims=True) + EPS) * g * (1.0 - lam_init)


def _const_spec(shape):
    return pl.BlockSpec(shape, lambda *_: (0,) * len(shape), pipeline_mode=pl.Buffered(1))


def _params(n_grid):
    return pltpu.CompilerParams(dimension_semantics=("arbitrary",) * n_grid,
                                vmem_limit_bytes=VMEM_LIMIT_BYTES)


def _qkv_kernel(x_ref, g_ref, w_ref, kf_ref, vf_ref, kb_ref, qt_ref, vt_ref):
    hb = _rms(x_ref[...], g_ref[...]).astype(BF16)
    w = ATTN_WIDTH
    tm = hb.shape[0]
    q = jnp.dot(hb, w_ref[:, 0:w], preferred_element_type=F32)
    qt_ref[...] = (q * (SCALE * LOG2E)).T.astype(BF16).reshape(N_HEADS, QK_DIM, tm)
    k = jnp.dot(hb, w_ref[:, w:2 * w], preferred_element_type=F32)
    kf_ref[...] = k
    for h in range(N_HEADS):
        kb_ref[h] = k[:, h * QK_DIM:(h + 1) * QK_DIM].astype(BF16)
    v = jnp.dot(hb, w_ref[:, 2 * w:3 * w], preferred_element_type=F32)
    vf_ref[...] = v
    vt_ref[:, 0:V_DIM, :] = v.T.astype(BF16).reshape(N_HEADS, V_DIM, tm)
    vt_ref[:, V_DIM:V_EXT, :] = jnp.ones((N_HEADS, V_EXT - V_DIM, tm), BF16)


def _qkv_call(x, g, w_bf16, tm, seq):
    m = x.shape[0]
    tiles_per_seq = seq // tm
    row = lambda width: pl.BlockSpec((tm, width), lambda i: (i, 0))
    fmaj = lambda rows: pl.BlockSpec((None, N_HEADS, rows, tm),
                                     lambda i: (i // tiles_per_seq, 0, 0, i % tiles_per_seq))
    hmaj = pl.BlockSpec((None, N_HEADS, tm, QK_DIM),
                        lambda i: (i // tiles_per_seq, 0, i % tiles_per_seq, 0))
    return pl.pallas_call(
        _qkv_kernel,
        grid=(m // tm,),
        in_specs=[row(D_MODEL), _const_spec((1, D_MODEL)), _const_spec(w_bf16.shape)],
        out_specs=[row(ATTN_WIDTH)] * 2 + [hmaj, fmaj(QK_DIM), fmaj(V_EXT)],
        out_shape=[jax.ShapeDtypeStruct((m, ATTN_WIDTH), F32)] * 2
        + [jax.ShapeDtypeStruct((m // seq, N_HEADS, seq, QK_DIM), BF16),
           jax.ShapeDtypeStruct((m // seq, N_HEADS, QK_DIM, seq), BF16),
           jax.ShapeDtypeStruct((m // seq, N_HEADS, V_EXT, seq), BF16)],
        compiler_params=_params(1),
        name="qkv_proj",
    )(x, g, w_bf16)


def _prompt_attn_kernel(relb_ref, lam_ref, g_ref, qt_ref, k_ref, vt_ref, o_ref,
                        q2_ref, m_ref, acc_ref, bias_ref, s0_ref, s1_ref, p0_ref, p1_ref,
                        a0_ref, a1_ref, t0_ref, t1_ref, *, tq, tk, lam_init):
    head = pl.program_id(0)
    qi = pl.program_id(1)
    chains = range(qt_ref.shape[0])
    ratio = tq // tk
    n_band = ratio + 1
    last = ratio * (qi + 1) - 1

    @pl.when(qi == 0)
    def _build_bias():
        j = lax.broadcasted_iota(jnp.int32, (tk, tq), 0)
        i = lax.broadcasted_iota(jnp.int32, (tk, tq), 1)
        for d in range(n_band):
            n = i - j + tk * (d + 1 - ratio)
            bias_ref[d] = jnp.where(n >= 0, _shifted_bias(n, relb_ref, head), NEG_INF)
        bias_ref[n_band] = jnp.zeros((tk, tq), F32)

    feat = lax.broadcasted_iota(jnp.int32, (QK_DIM, tq), 0)
    for c in chains:
        qt = qt_ref[c]
        zero = jnp.zeros_like(qt)
        q2_ref[c, :, 0:tq] = jnp.where(feat < HEAD_DIM, qt, zero)
        q2_ref[c, :, tq:2 * tq] = jnp.where(feat >= HEAD_DIM, qt, zero)
    m_ref[...] = jnp.full(m_ref.shape, NEG_INF, F32)
    acc_ref[...] = jnp.zeros(acc_ref.shape, F32)
    for p_buf, a_buf in ((p0_ref, a0_ref), (p1_ref, a1_ref)):
        p_buf[...] = jnp.zeros(p_buf.shape, BF16)
        a_buf[...] = jnp.ones(a_buf.shape, F32)

    @pl.when((head == 0) & (qi == 0))
    def _first_touch():
        for s_buf, t_buf in ((s0_ref, t0_ref), (s1_ref, t1_ref)):
            s_buf[...] = jnp.zeros(s_buf.shape, F32)
            t_buf[...] = jnp.zeros(t_buf.shape, F32)

    def time_step(tau, parity, *, scores=True, softmax=True, with_bias=False):
        s_buf, t_buf = (s0_ref, t0_ref) if parity == 0 else (s1_ref, t1_ref)
        p_buf, a_buf = (p0_ref, a0_ref) if parity == 0 else (p1_ref, a1_ref)
        jv = pl.multiple_of(jnp.clip(tau - 4, 0, last) * tk, tk)
        valid = (tau >= 2) & (tau - 2 <= last)
        jq = jnp.minimum(tau, last)
        start = pl.multiple_of(jq * tk, tk)
        if scores and with_bias:
            b = bias_ref[jnp.minimum(last - jq, n_band)]
            b2 = jnp.concatenate([b, b], axis=1)

        def values_stage(c):
            pv = jnp.dot(vt_ref[c, :, pl.ds(jv, tk)], p_buf[c], preferred_element_type=F32)
            acc_ref[c] = a_buf[c] * acc_ref[c] + pv

        def softmax_stage(c):
            m_prev = m_ref[c]
            m_new = jnp.where(valid, jnp.maximum(m_prev, t_buf[c]), m_prev)
            a_buf[c] = jnp.exp2(m_prev - m_new)
            p_buf[c] = jnp.exp2((s_buf[c] - jnp.where(valid, m_new, -NEG_INF)).astype(BF16))
            m_ref[c] = m_new

        def scores_stage(c):
            s_new = jnp.dot(k_ref[c, pl.ds(start, tk), :], q2_ref[c],
                            preferred_element_type=F32)
            if with_bias:
                s_new = s_new + b2
            s_buf[c] = s_new
            t_buf[c] = jnp.max(s_new, axis=0, keepdims=True)

        for c in chains:
            values_stage(c)
            if softmax:
                softmax_stage(c)
            if scores:
                scores_stage(c)

    n_far = jnp.maximum(last + 1 - n_band, 0)
    n_pairs = n_far // 2
    done = 0
    for pairs_per_trip in FAR_TRIP_PAIRS:
        def far_trip(u, carry, pairs_per_trip=pairs_per_trip):
            for d in range(2 * pairs_per_trip):
                time_step(2 * u * pairs_per_trip + d, d % 2)
            return carry

        trips = n_pairs // pairs_per_trip
        lax.fori_loop(done // pairs_per_trip, trips, far_trip, 0)
        done = trips * pairs_per_trip
    tau0 = 2 * n_pairs
    n_tail = n_band + 1
    for d in range(n_tail + 4):
        time_step(tau0 + d, d % 2, scores=d < n_tail, softmax=d < n_tail + 2, with_bias=True)

    lam = _diff_lambda(lam_ref, lam_init)
    gain = g_ref[...] * (1.0 - lam_init)
    for c in chains:
        inv_l = 1.0 / acc_ref[c, V_DIM:V_DIM + 1, :]
        o1 = acc_ref[c, 0:V_DIM, 0:tq] * inv_l[:, 0:tq]
        o2 = acc_ref[c, 0:V_DIM, tq:2 * tq] * inv_l[:, tq:2 * tq]
        ot = o1 - lam * o2
        ms = jnp.mean(ot * ot, axis=0, keepdims=True)
        o_ref[c] = (ot * lax.rsqrt(ms + EPS) * gain).T.astype(BF16)


def _prompt_attn_call(qt, kb, vt, rel_bias, lam_vecs, subln_g_col, lam_init, tq, tk):
    b, _, _, s = qt.shape
    assert s % tq == 0 and tq % tk == 0 and tk >= FAR_DISTANCE
    out_tile = pl.BlockSpec((b, None, tq, V_DIM), lambda h, qi: (0, h, qi, 0))
    q_tile = pl.BlockSpec((b, None, QK_DIM, tq), lambda h, qi: (0, h, 0, qi))
    k_seq = pl.BlockSpec((b, None, s, QK_DIM), lambda h, qi: (0, h, 0, 0))
    v_seq = pl.BlockSpec((b, None, V_EXT, s), lambda h, qi: (0, h, 0, 0))
    stat = pltpu.VMEM((b, 1, 2 * tq), F32)
    return pl.pallas_call(
        functools.partial(_prompt_attn_kernel, tq=tq, tk=tk, lam_init=lam_init),
        grid=(N_HEADS, s // tq),
        in_specs=[pl.BlockSpec(memory_space=pltpu.SMEM), _const_spec((4, HEAD_DIM)),
                  _const_spec((V_DIM, 1)), q_tile, k_seq, v_seq],
        out_specs=out_tile,
        out_shape=jax.ShapeDtypeStruct((b, N_HEADS, s, V_DIM), BF16),
        scratch_shapes=[pltpu.VMEM((b, QK_DIM, 2 * tq), BF16), stat,
                        pltpu.VMEM((b, V_EXT, 2 * tq), F32),
                        pltpu.VMEM((tq // tk + 2, tk, tq), F32),
                        pltpu.VMEM((b, tk, 2 * tq), F32), pltpu.VMEM((b, tk, 2 * tq), F32),
                        pltpu.VMEM((b, tk, 2 * tq), BF16), pltpu.VMEM((b, tk, 2 * tq), BF16),
                        stat, stat, stat, stat],
        compiler_params=_params(2),
        name="prompt_attn",
    )(rel_bias, lam_vecs, subln_g_col, qt, kb, vt)


def _sample_attn_kernel(pt_ref, relb_ref, lam_ref, g_ref, q_ref, kn_ref, vn_ref, *rest,
                        dec_seq, n_steps, group, lam_init):
    del pt_ref
    kc_refs, vc_refs = rest[:group], rest[group:2 * group]
    o_ref, m_ref, l_ref, acc_ref, qm_ref, mask_ref, lastb_ref, newb_ref = rest[2 * group:]
    step = pl.program_id(1)
    rows = N_HEADS * 2 * dec_seq
    per_head = 2 * dec_seq
    assert per_head == V7X_SUBLANES
    flat = PAGE_SIZE * N_HEADS
    nt = (((1,), (1,)), ((), ()))

    def grids(width):
        r = lax.broadcasted_iota(jnp.int32, (rows, width), 0)
        c = lax.broadcasted_iota(jnp.int32, (rows, width), 1)
        same_head = (c % N_HEADS) == (r // per_head)
        return r % dec_seq, c // N_HEADS, same_head

    def head_bias(dist):
        parts = [_shifted_bias(dist[h * per_head:(h + 1) * per_head], relb_ref, h)
                 for h in range(N_HEADS)]
        return jnp.concatenate(parts, axis=0)

    @pl.when((pl.program_id(0) == 0) & (step == 0))
    def _tables():
        tq, tk, ok = grids(flat)
        mask_ref[...] = jnp.where(ok, 0.0, NEG_INF)
        lastb_ref[...] = jnp.where(ok, head_bias(PAGE_SIZE + tq - tk), NEG_INF)
        tqn, tkn, okn = grids(kn_ref.shape[0])
        newb_ref[...] = jnp.where(okn & (tkn <= tqn), head_bias(tqn - tkn), NEG_INF)

    @pl.when(step == 0)
    def _init():
        m_ref[...] = jnp.full(m_ref.shape, NEG_INF, F32)
        l_ref[...] = jnp.zeros(l_ref.shape, F32)
        acc_ref[...] = jnp.zeros(acc_ref.shape, F32)
        q = q_ref[...]
        row = lax.broadcasted_iota(jnp.int32, (rows, QK_DIM), 0)
        lane = lax.broadcasted_iota(jnp.int32, (rows, QK_DIM), 1)
        first_comp = (row // dec_seq) % 2 == 0
        qm_ref[...] = jnp.where((lane < HEAD_DIM) == first_comp, q, jnp.zeros_like(q))

    def update(s_list, v_list):
        m_prev = m_ref[...]
        m_tile = functools.reduce(jnp.maximum, [jnp.max(s, axis=1, keepdims=True) for s in s_list])
        m_new = jnp.maximum(m_prev, m_tile)
        alpha = jnp.exp2(m_prev - m_new)
        p_list = [jnp.exp2(s - m_new[:, 0:1]) for s in s_list]
        l_ref[...] = alpha * l_ref[...] + sum(jnp.sum(p, axis=1, keepdims=True) for p in p_list)
        pv = sum(jnp.dot(p.astype(BF16), v(), preferred_element_type=F32)
                 for p, v in zip(p_list, v_list))
        acc_ref[...] = alpha * acc_ref[...] + pv
        m_ref[...] = m_new

    qm = qm_ref[...]

    def scores(keys, bias_ref):
        return lax.dot_general(qm, keys, nt, preferred_element_type=F32) + bias_ref[...]

    def page_update(lo, hi, last_bias_ref):
        biases = [mask_ref] * (hi - lo - 1) + [last_bias_ref]
        update([scores(kc[...].reshape(flat, QK_DIM).astype(BF16), b)
                for kc, b in zip(kc_refs[lo:hi], biases)],
               [lambda vc=vc: vc[...].reshape(flat, V_DIM).astype(BF16) for vc in vc_refs[lo:hi]])

    @pl.when(step < n_steps - 1)
    def _far_pages():
        page_update(0, group, mask_ref)

    @pl.when(step == n_steps - 1)
    def _last_pages():
        page_update(0, group, lastb_ref)
        update([scores(kn_ref[...], newb_ref)], [lambda: vn_ref[...]])
        o = acc_ref[...] / l_ref[...]
        o2 = pltpu.roll(o, rows - dec_seq, axis=0)
        od = o - _diff_lambda(lam_ref, lam_init) * o2
        o_ref[...] = _head_norm(od, g_ref[...], lam_init)


def _sample_attn_call(q_rows, k_new, v_new, cache_k, cache_v, page_table, rel_bias, lam_vecs,
                      subln_g, lam_init, dec_seq):
    nb, rows, _ = q_rows.shape
    n_pages = page_table.shape[1]
    group = SAMPLE_PAGES_PER_STEP
    assert PAGE_SIZE >= FAR_DISTANCE + dec_seq and n_pages % group == 0
    n_steps = n_pages // group
    flat = PAGE_SIZE * N_HEADS
    per_b = lambda shape: pl.BlockSpec((None,) + shape, lambda b, p, pt: (b, 0, 0))
    paged = lambda g: pl.BlockSpec((None, PAGE_SIZE, N_HEADS, QK_DIM),
                                   lambda b, p, pt: (pt[b, p * group + g], 0, 0, 0))
    const = lambda shape: pl.BlockSpec(shape, lambda b, p, pt: (0,) * len(shape))
    grid_spec = pltpu.PrefetchScalarGridSpec(
        num_scalar_prefetch=1,
        grid=(nb, n_steps),
        in_specs=[pl.BlockSpec(memory_space=pltpu.SMEM), const((4, HEAD_DIM)), const((1, V_DIM)),
                  per_b((rows, QK_DIM)), per_b(k_new.shape[1:]), per_b(v_new.shape[1:])]
        + [paged(g) for g in range(group)] * 2,
        out_specs=per_b((rows, V_DIM)),
        scratch_shapes=[pltpu.VMEM((rows, V7X_LANES), F32),
                        pltpu.VMEM((rows, V7X_LANES), F32),
                        pltpu.VMEM((rows, V_DIM), F32),
                        pltpu.VMEM((rows, QK_DIM), BF16),
                        pltpu.VMEM((rows, flat), F32),
                        pltpu.VMEM((rows, flat), F32),
                        pltpu.VMEM((rows, k_new.shape[1]), F32)],
    )
    return pl.pallas_call(
        functools.partial(_sample_attn_kernel, dec_seq=dec_seq, n_steps=n_steps, group=group,
                          lam_init=lam_init),
        grid_spec=grid_spec,
        out_shape=jax.ShapeDtypeStruct((nb, rows, V_DIM), F32),
        compiler_params=_params(2),
        name="sample_attn",
    )(page_table, rel_bias, lam_vecs, subln_g, q_rows, k_new, v_new,
      *([cache_k] * group), *([cache_v] * group))


def _ffn_kernel(*refs, d_ff, with_proj):
    if with_proj:
        x_ref, a_ref, wo_ref, gmix_ref, gpre_ref, wgu_ref, wd_ref, gpost_ref, o_ref = refs
        a = jnp.concatenate([a_ref[h] for h in range(N_HEADS)], axis=1)
        x = x_ref[...] + _rms(jnp.dot(a, wo_ref[...], preferred_element_type=F32), gmix_ref[...])
    else:
        x_ref, gpre_ref, wgu_ref, wd_ref, gpost_ref, o_ref = refs
        x = x_ref[...]
    hb = _rms(x, gpre_ref[...]).astype(BF16)
    g = jnp.dot(hb, wgu_ref[:, 0:d_ff], preferred_element_type=F32)
    u = jnp.dot(hb, wgu_ref[:, d_ff:2 * d_ff], preferred_element_type=F32)
    a = (g * (1.0 / (1.0 + jnp.exp(-g))) * u).astype(BF16)
    y = jnp.dot(a, wd_ref[...], preferred_element_type=F32)
    o_ref[...] = x + _rms(y, gpost_ref[...])


def _ffn_call(x, g_pre, wgu_bf16, wd_bf16, g_post, tm, proj=None):
    m = x.shape[0]
    d_ff = wd_bf16.shape[0]
    row = lambda width: pl.BlockSpec((tm, width), lambda i: (i, 0))
    ffn_specs = [_const_spec((1, D_MODEL)), _const_spec(wgu_bf16.shape),
                 _const_spec(wd_bf16.shape), _const_spec((1, D_MODEL))]
    ffn_args = (g_pre, wgu_bf16, wd_bf16, g_post)
    if proj is None:
        specs, args = [row(D_MODEL)] + ffn_specs, (x,) + ffn_args
    else:
        a_bf16, wo_bf16, g_mix = proj
        tiles_per_seq = a_bf16.shape[2] // tm
        heads = pl.BlockSpec((None, N_HEADS, tm, V_DIM),
                             lambda i: (i // tiles_per_seq, 0, i % tiles_per_seq, 0))
        specs = [row(D_MODEL), heads, _const_spec(wo_bf16.shape),
                 _const_spec((1, D_MODEL))] + ffn_specs
        args = (x, a_bf16, wo_bf16, g_mix) + ffn_args
    return pl.pallas_call(
        functools.partial(_ffn_kernel, d_ff=d_ff, with_proj=proj is not None),
        grid=(m // tm,),
        in_specs=specs,
        out_specs=row(D_MODEL),
        out_shape=jax.ShapeDtypeStruct((m, D_MODEL), F32),
        compiler_params=_params(1),
        name="swiglu_ffn",
    )(*args)


def _conv_kernel(*refs, tm, seq):
    carry_mode = seq >= tm
    if carry_mode:
        (x_ref, gpre_ref, win_ref, cw_ref, wout_ref, gpost_ref, st_ref,
         o_ref, tail_ref, carry_ref) = refs
    else:
        (x_ref, gpre_ref, win_ref, cw_ref, wout_ref, gpost_ref, inj1_ref, inj2_ref,
         o_ref, u_ref) = refs
    d = D_MODEL
    x = x_ref[...]
    hb = _rms(x, gpre_ref[...]).astype(BF16)
    gc = jnp.dot(hb, win_ref[:, d:2 * d], preferred_element_type=F32)
    xt = jnp.dot(hb, win_ref[:, 2 * d:3 * d], preferred_element_type=F32)
    u = gc * xt
    row = lax.broadcasted_iota(jnp.int32, (tm, d), 0)
    r1 = pltpu.roll(u, 1, axis=0)
    r2 = pltpu.roll(u, 2, axis=0)
    if carry_mode:
        i = pl.program_id(0)
        tiles_per_seq = seq // tm

        @pl.when(i % tiles_per_seq == 0)
        def _seq_start():
            carry_ref[...] = st_ref[...]

        c = carry_ref[...]
        c1 = c[V7X_SUBLANES - 1:V7X_SUBLANES]
        c2 = c[V7X_SUBLANES - 2:V7X_SUBLANES - 1]
        prev1 = jnp.where(row == 0, c1, r1)
        prev2 = jnp.where(row == 0, c2, jnp.where(row == 1, c1, r2))
        tail = u[tm - V7X_SUBLANES:tm]
        carry_ref[...] = tail
        tail_ref[...] = tail
    else:
        t = row % seq
        prev1 = jnp.where(t >= 1, r1, inj1_ref[...])
        prev2 = jnp.where(t >= 2, r2, inj2_ref[...])
        u_ref[...] = u
    cw = cw_ref[...]
    y = cw[0:1] * prev2 + cw[1:2] * prev1 + cw[2:3] * u
    gb = jnp.dot(hb, win_ref[:, 0:d], preferred_element_type=F32)
    z = jnp.dot((gb * y).astype(BF16), wout_ref[...], preferred_element_type=F32)
    o_ref[...] = x + _rms(z, gpost_ref[...])


def _conv_call(x, g_pre, win_bf16, conv_w, wout_bf16, g_post, state, tm, seq):
    m = x.shape[0]
    nb = m // seq
    row = pl.BlockSpec((tm, D_MODEL), lambda i: (i, 0))
    common = [row, _const_spec((1, D_MODEL)), _const_spec(win_bf16.shape),
              _const_spec((CONV_WIDTH, D_MODEL)), _const_spec(wout_bf16.shape),
              _const_spec((1, D_MODEL))]
    kern = functools.partial(_conv_kernel, tm=tm, seq=seq)
    if seq >= tm:
        assert seq % tm == 0
        n_tiles = m // tm
        tiles_per_seq = seq // tm
        pad = jnp.zeros((nb, V7X_SUBLANES - (CONV_WIDTH - 1), D_MODEL), F32)
        st = jnp.concatenate([pad, state], axis=1)
        st_spec = pl.BlockSpec((None, V7X_SUBLANES, D_MODEL), lambda i: (i // tiles_per_seq, 0, 0))
        tail_spec = pl.BlockSpec((None, V7X_SUBLANES, D_MODEL), lambda i: (i, 0, 0))
        out, tails = pl.pallas_call(
            kern, grid=(n_tiles,),
            in_specs=common + [st_spec],
            out_specs=[row, tail_spec],
            out_shape=[jax.ShapeDtypeStruct((m, D_MODEL), F32),
                       jax.ShapeDtypeStruct((n_tiles, V7X_SUBLANES, D_MODEL), F32)],
            scratch_shapes=[pltpu.VMEM((V7X_SUBLANES, D_MODEL), F32)],
            compiler_params=_params(1),
            name="short_conv_seq",
        )(x, g_pre, win_bf16, conv_w, wout_bf16, g_post, st)
        last = tails.reshape(nb, tiles_per_seq, V7X_SUBLANES, D_MODEL)[:, -1]
        return out, last[:, V7X_SUBLANES - (CONV_WIDTH - 1):]
    assert tm % seq == 0 and m == tm and seq >= CONV_WIDTH - 1
    zeros = lambda n: jnp.zeros((nb, n, D_MODEL), F32)
    inj1 = jnp.concatenate([state[:, 1:2], zeros(seq - 1)], axis=1).reshape(m, D_MODEL)
    inj2 = jnp.concatenate([state[:, 0:2], zeros(seq - 2)], axis=1).reshape(m, D_MODEL)
    out, u = pl.pallas_call(
        kern, grid=(1,),
        in_specs=common + [row, row],
        out_specs=[row, row],
        out_shape=[jax.ShapeDtypeStruct((m, D_MODEL), F32)] * 2,
        compiler_params=_params(1),
        name="short_conv_tok",
    )(x, g_pre, win_bf16, conv_w, wout_bf16, g_post, inj1, inj2)
    return out, u.reshape(nb, seq, D_MODEL)[:, seq - (CONV_WIDTH - 1):]


def kernel(x_prompt, x_sample, cache_k, cache_v, state_conv, page_table, rel_bias, w_qkv, w_o, lambda_q1, lambda_k1, lambda_q2, lambda_k2, subln_g, w_conv_in, conv_w, w_conv_out, norm_mix_pre, norm_mix_post, norm_ffn_pre, norm_ffn_post, w_ffn_gu, w_ffn_down):
    b, s, d = x_prompt.shape
    nb, dec_seq, _ = x_sample.shape
    assert d == D_MODEL and page_table.shape[1] * PAGE_SIZE >= FAR_DISTANCE
    mp, ms = b * s, nb * dec_seq
    tm_p = TOKEN_TILE

    wqkv = w_qkv.astype(BF16)
    wo = w_o.astype(BF16)
    win = w_conv_in.astype(BF16)
    wout = w_conv_out.astype(BF16)
    wgu = w_ffn_gu.astype(BF16)
    wd = w_ffn_down.astype(BF16)
    gain = lambda g, i: g[i].reshape(1, D_MODEL)
    lam_vecs = jnp.stack([lambda_q1, lambda_k1, lambda_q2, lambda_k2])
    sub_g = subln_g.reshape(1, V_DIM)
    lam0 = _lambda_init(0)

    xp = x_prompt.reshape(mp, d)
    xs = x_sample.reshape(ms, d)

    g_pre = gain(norm_mix_pre, 0)
    kf_p, vf_p, kb_p, qt_p, vt_p = _qkv_call(xp, g_pre, wqkv, tm_p, s)
    kf_s, vf_s, kb_s, qt_s, _ = _qkv_call(xs, g_pre, wqkv, ms, ms)

    on_p = _prompt_attn_call(qt_p, kb_p, vt_p, rel_bias, lam_vecs,
                             subln_g.reshape(V_DIM, 1), lam0, ATTN_QUERY_TILE, ATTN_KEY_TILE)

    q4 = qt_s.reshape(N_HEADS, QK_DIM, nb, dec_seq).transpose(2, 0, 3, 1)
    q_rows = jnp.stack([q4, q4], axis=2).reshape(nb, N_HEADS * 2 * dec_seq, QK_DIM)
    pad_tok = PAGE_SIZE // N_HEADS - dec_seq
    flat_new = lambda a: jnp.pad(a.reshape(nb, dec_seq, N_HEADS, QK_DIM),
                                 ((0, 0), (0, pad_tok), (0, 0), (0, 0))).reshape(nb, PAGE_SIZE, QK_DIM)
    on_rows = _sample_attn_call(q_rows, flat_new(kb_s[0].transpose(1, 0, 2)),
                                flat_new(vf_s.astype(BF16)), cache_k, cache_v,
                                page_table, rel_bias, lam_vecs, sub_g, lam0, dec_seq)
    on_s = on_rows.reshape(nb, N_HEADS, 2, dec_seq, V_DIM)[:, :, 0].transpose(1, 0, 2, 3)
    on_s = on_s.reshape(1, N_HEADS, ms, V_DIM).astype(BF16)

    g_post = gain(norm_mix_post, 0)
    ffn = lambda x, i, tm, proj=None: _ffn_call(x, gain(norm_ffn_pre, i), wgu[i], wd[i],
                                                gain(norm_ffn_post, i), tm, proj)
    xp = ffn(xp, 0, tm_p, (on_p, wo, g_post))
    xs = ffn(xs, 0, ms, (on_s, wo, g_post))

    g_pre, g_post = gain(norm_mix_pre, 1), gain(norm_mix_post, 1)
    zero_state = jnp.zeros((b, CONV_WIDTH - 1, d), F32)
    xp, conv_p = _conv_call(xp, g_pre, win, conv_w, wout, g_post, zero_state, tm_p, s)
    xs, conv_s = _conv_call(xs, g_pre, win, conv_w, wout, g_post, state_conv, ms, dec_seq)
    xp = ffn(xp, 1, tm_p)
    xs = ffn(xs, 1, ms)

    return (xp.reshape(b, s, d), xs.reshape(nb, dec_seq, d),
            kf_p.reshape(b, s, N_HEADS, QK_DIM), vf_p.reshape(b, s, N_HEADS, V_DIM),
            kf_s.reshape(nb, dec_seq, N_HEADS, QK_DIM), vf_s.reshape(nb, dec_seq, N_HEADS, V_DIM),
            conv_p, conv_s)
```

```python
import functools
import math

import jax
import jax.numpy as jnp
from jax import lax
from jax.experimental import pallas as pl
from jax.experimental.pallas import tpu as pltpu

D_MODEL = 1024
N_HEADS = 8
HEAD_DIM = 64
QK_DIM = 2 * HEAD_DIM
V_DIM = 2 * HEAD_DIM
ATTN_WIDTH = N_HEADS * V_DIM
N_BUCKETS = 32
MAX_DISTANCE = 128
CONV_WIDTH = 3
PAGE_SIZE = 128
EPS = 1e-6
NEG_INF = -1e30
SCALE = HEAD_DIM ** -0.5
LOG2E = math.log2(math.e)

V7X_LANES = 128
V7X_BF16_SUBLANES = 16
V_EXT = V_DIM + V7X_BF16_SUBLANES
V7X_SUBLANES = 8
VMEM_LIMIT_BYTES = 56 * 1024 * 1024
SAMPLE_PAGES_PER_UPDATE = 8
ATTN_VMEM_LIMIT_BYTES = 60 * 1024 * 1024
TOKEN_TILE = 512
ATTN_QUERY_TILE = 512
ATTN_KEY_TILE = 256
FAR_TRIP_PAIRS = (2, 1)
F32 = jnp.float32
BF16 = jnp.bfloat16


def _lambda_init(layer):
    return 0.8 - 0.6 * math.exp(-0.3 * layer)


def _bucket_uppers():
    max_exact = N_BUCKETS // 2
    n_far = None
    buckets = []
    for n in range(0, 4 * MAX_DISTANCE):
        if n < max_exact:
            b = n
        else:
            r = math.log(n / max_exact) / math.log(MAX_DISTANCE / max_exact) * (N_BUCKETS - max_exact)
            frac = r - math.floor(r)
            assert min(frac, 1 - frac) > 1e-4 or n == max_exact or n >= MAX_DISTANCE, n
            b = min(max_exact + int(r + (1e-6 if n == max_exact else 0.0)), N_BUCKETS - 1)
            if n >= MAX_DISTANCE:
                b = N_BUCKETS - 1
        buckets.append(b)
    uppers = []
    for b in range(N_BUCKETS - 1):
        uppers.append(next(n for n, bb in enumerate(buckets) if bb > b))
    n_far = uppers[-1]
    return tuple(uppers), n_far


BUCKET_UPPERS, FAR_DISTANCE = _bucket_uppers()


def _shifted_bias(n, relb_ref, head):
    far = relb_ref[N_BUCKETS - 1, head]
    val = jnp.zeros(n.shape, F32)
    for b in range(N_BUCKETS - 2, -1, -1):
        val = jnp.where(n < BUCKET_UPPERS[b], (relb_ref[b, head] - far) * LOG2E, val)
    return val


def _diff_lambda(lam_ref, lam_init):
    lv = lam_ref[...]
    a = jnp.sum(lv[0:1] * lv[1:2], axis=1, keepdims=True)
    b = jnp.sum(lv[2:3] * lv[3:4], axis=1, keepdims=True)
    return jnp.exp(a) - jnp.exp(b) + lam_init


def _rms(x, g):
    return x * lax.rsqrt(jnp.mean(x * x, axis=-1, keepdims=True) + EPS) * g


def _head_norm(o, g, lam_init):
    return o * lax.rsqrt(jnp.mean(o * o, axis=-1, keepdims=True) + EPS) * g * (1.0 - lam_init)


def _const_spec(shape):
    return pl.BlockSpec(shape, lambda *_: (0,) * len(shape), pipeline_mode=pl.Buffered(1))


def _params(n_grid):
    return pltpu.CompilerParams(dimension_semantics=("arbitrary",) * n_grid,
                                vmem_limit_bytes=VMEM_LIMIT_BYTES)


def _qkv_kernel(x_ref, g_ref, w_ref, kf_ref, vf_ref, kb_ref, qt_ref, vt_ref):
    hb = _rms(x_ref[...], g_ref[...]).astype(BF16)
    w = ATTN_WIDTH
    tm = hb.shape[0]
    q = jnp.dot(hb, w_ref[:, 0:w], preferred_element_type=F32)
    qt_ref[...] = (q * (SCALE * LOG2E)).T.astype(BF16).reshape(N_HEADS, QK_DIM, tm)
    k = jnp.dot(hb, w_ref[:, w:2 * w], preferred_element_type=F32)
    kf_ref[...] = k
    for h in range(N_HEADS):
        kb_ref[h] = k[:, h * QK_DIM:(h + 1) * QK_DIM].astype(BF16)
    v = jnp.dot(hb, w_ref[:, 2 * w:3 * w], preferred_element_type=F32)
    vf_ref[...] = v
    vt_ref[:, 0:V_DIM, :] = v.T.astype(BF16).reshape(N_HEADS, V_DIM, tm)
    vt_ref[:, V_DIM:V_EXT, :] = jnp.ones((N_HEADS, V_EXT - V_DIM, tm), BF16)


def _qkv_call(x, g, w_bf16, tm, seq):
    m = x.shape[0]
    tiles_per_seq = seq // tm
    row = lambda width: pl.BlockSpec((tm, width), lambda i: (i, 0))
    fmaj = lambda rows: pl.BlockSpec((None, N_HEADS, rows, tm),
                                     lambda i: (i // tiles_per_seq, 0, 0, i % tiles_per_seq))
    hmaj = pl.BlockSpec((None, N_HEADS, tm, QK_DIM),
                        lambda i: (i // tiles_per_seq, 0, i % tiles_per_seq, 0))
    return pl.pallas_call(
        _qkv_kernel,
        grid=(m // tm,),
        in_specs=[row(D_MODEL), _const_spec((1, D_MODEL)), _const_spec(w_bf16.shape)],
        out_specs=[row(ATTN_WIDTH)] * 2 + [hmaj, fmaj(QK_DIM), fmaj(V_EXT)],
        out_shape=[jax.ShapeDtypeStruct((m, ATTN_WIDTH), F32)] * 2
        + [jax.ShapeDtypeStruct((m // seq, N_HEADS, seq, QK_DIM), BF16),
           jax.ShapeDtypeStruct((m // seq, N_HEADS, QK_DIM, seq), BF16),
           jax.ShapeDtypeStruct((m // seq, N_HEADS, V_EXT, seq), BF16)],
        compiler_params=_params(1),
        name="qkv_proj",
    )(x, g, w_bf16)


def _prompt_tile(relb_ref, lam_ref, g_ref, qt_ref, k_ref, vt_ref, o_ref,
                 q2_ref, m_ref, acc_ref, bias_ref, s0_ref, s1_ref, p0_ref, p1_ref,
                 a0_ref, a1_ref, t0_ref, t1_ref, *, tq, tk, lam_init, head, qi):
    chains = range(qt_ref.shape[0])
    ratio = tq // tk
    n_band = ratio + 1
    last = ratio * (qi + 1) - 1

    @pl.when(qi == 0)
    def _build_bias():
        j = lax.broadcasted_iota(jnp.int32, (tk, tq), 0)
        i = lax.broadcasted_iota(jnp.int32, (tk, tq), 1)
        for d in range(n_band):
            n = i - j + tk * (d + 1 - ratio)
            bias_ref[d] = jnp.where(n >= 0, _shifted_bias(n, relb_ref, head), NEG_INF)
        bias_ref[n_band] = jnp.zeros((tk, tq), F32)

    feat = lax.broadcasted_iota(jnp.int32, (QK_DIM, tq), 0)
    for c in chains:
        qt = qt_ref[c]
        zero = jnp.zeros_like(qt)
        q2_ref[c, :, 0:tq] = jnp.where(feat < HEAD_DIM, qt, zero)
        q2_ref[c, :, tq:2 * tq] = jnp.where(feat >= HEAD_DIM, qt, zero)
    m_ref[...] = jnp.full(m_ref.shape, NEG_INF, F32)
    acc_ref[...] = jnp.zeros(acc_ref.shape, F32)
    for p_buf, a_buf in ((p0_ref, a0_ref), (p1_ref, a1_ref)):
        p_buf[...] = jnp.zeros(p_buf.shape, BF16)
        a_buf[...] = jnp.ones(a_buf.shape, F32)

    @pl.when((head == 0) & (qi == 0))
    def _first_touch():
        for s_buf, t_buf in ((s0_ref, t0_ref), (s1_ref, t1_ref)):
            s_buf[...] = jnp.zeros(s_buf.shape, F32)
            t_buf[...] = jnp.zeros(t_buf.shape, F32)

    def time_step(tau, parity, *, scores=True, softmax=True, with_bias=False):
        s_buf, t_buf = (s0_ref, t0_ref) if parity == 0 else (s1_ref, t1_ref)
        p_buf, a_buf = (p0_ref, a0_ref) if parity == 0 else (p1_ref, a1_ref)
        jv = pl.multiple_of(jnp.clip(tau - 4, 0, last) * tk, tk)
        valid = (tau >= 2) & (tau - 2 <= last)
        jq = jnp.minimum(tau, last)
        start = pl.multiple_of(jq * tk, tk)
        if scores and with_bias:
            b = bias_ref[jnp.minimum(last - jq, n_band)]
            b2 = jnp.concatenate([b, b], axis=1)

        def values_stage(c):
            pv = jnp.dot(vt_ref[c, :, pl.ds(jv, tk)], p_buf[c], preferred_element_type=F32)
            acc_ref[c] = a_buf[c] * acc_ref[c] + pv

        def softmax_stage(c):
            m_prev = m_ref[c]
            m_new = jnp.where(valid, jnp.maximum(m_prev, t_buf[c]), m_prev)
            a_buf[c] = jnp.exp2(m_prev - m_new)
            p_buf[c] = jnp.exp2((s_buf[c] - jnp.where(valid, m_new, -NEG_INF)).astype(BF16))
            m_ref[c] = m_new

        def scores_stage(c):
            s_new = jnp.dot(k_ref[c, pl.ds(start, tk), :], q2_ref[c],
                            preferred_element_type=F32)
            if with_bias:
                s_new = s_new + b2
            s_buf[c] = s_new
            t_buf[c] = jnp.max(s_new, axis=0, keepdims=True)

        for c in chains:
            values_stage(c)
            if softmax:
                softmax_stage(c)
            if scores:
                scores_stage(c)

    n_far = jnp.maximum(last + 1 - n_band, 0)
    n_pairs = n_far // 2
    done = 0
    for pairs_per_trip in FAR_TRIP_PAIRS:
        def far_trip(u, carry, pairs_per_trip=pairs_per_trip):
            for d in range(2 * pairs_per_trip):
                time_step(2 * u * pairs_per_trip + d, d % 2)
            return carry

        trips = n_pairs // pairs_per_trip
        lax.fori_loop(done // pairs_per_trip, trips, far_trip, 0)
        done = trips * pairs_per_trip
    tau0 = 2 * n_pairs
    n_tail = n_band + 1
    for d in range(n_tail + 4):
        time_step(tau0 + d, d % 2, scores=d < n_tail, softmax=d < n_tail + 2, with_bias=True)

    lam = _diff_lambda(lam_ref, lam_init)
    gain = g_ref[...] * (1.0 - lam_init)
    for c in chains:
        inv_l = 1.0 / acc_ref[c, V_DIM:V_DIM + 1, :]
        o1 = acc_ref[c, 0:V_DIM, 0:tq] * inv_l[:, 0:tq]
        o2 = acc_ref[c, 0:V_DIM, tq:2 * tq] * inv_l[:, tq:2 * tq]
        ot = o1 - lam * o2
        ms = jnp.mean(ot * ot, axis=0, keepdims=True)
        o_ref[c] = (ot * lax.rsqrt(ms + EPS) * gain).T.astype(BF16)


def _decode_pages(relb_ref, lam_ref, g_ref, q_ref, kn_ref, vn_ref, kc_refs, vc_refs, o_ref,
                  m_ref, l_ref, acc_ref, qm_ref, mask_ref, lastb_ref, newb_ref,
                  *, dec_seq, lam_init, pages_per_update, build_tables, first, last):
    group = len(kc_refs)
    rows = N_HEADS * 2 * dec_seq
    per_head = 2 * dec_seq
    assert per_head == V7X_SUBLANES
    flat = PAGE_SIZE * N_HEADS
    nt = (((1,), (1,)), ((), ()))

    def grids(width):
        r = lax.broadcasted_iota(jnp.int32, (rows, width), 0)
        c = lax.broadcasted_iota(jnp.int32, (rows, width), 1)
        same_head = (c % N_HEADS) == (r // per_head)
        return r % dec_seq, c // N_HEADS, same_head

    def head_bias(dist):
        parts = [_shifted_bias(dist[h * per_head:(h + 1) * per_head], relb_ref, h)
                 for h in range(N_HEADS)]
        return jnp.concatenate(parts, axis=0)

    @pl.when(build_tables)
    def _tables():
        tq, tk, ok = grids(flat)
        mask_ref[...] = jnp.where(ok, 0.0, NEG_INF)
        lastb_ref[...] = jnp.where(ok, head_bias(PAGE_SIZE + tq - tk), NEG_INF)
        tqn, tkn, okn = grids(kn_ref.shape[0])
        newb_ref[...] = jnp.where(okn & (tkn <= tqn), head_bias(tqn - tkn), NEG_INF)

    @pl.when(first)
    def _init():
        m_ref[...] = jnp.full(m_ref.shape, NEG_INF, F32)
        l_ref[...] = jnp.zeros(l_ref.shape, F32)
        acc_ref[...] = jnp.zeros(acc_ref.shape, F32)
        q = q_ref[...]
        row = lax.broadcasted_iota(jnp.int32, (rows, QK_DIM), 0)
        lane = lax.broadcasted_iota(jnp.int32, (rows, QK_DIM), 1)
        first_comp = (row // dec_seq) % 2 == 0
        qm_ref[...] = jnp.where((lane < HEAD_DIM) == first_comp, q, jnp.zeros_like(q))

    def update(s_list, v_list):
        m_prev = m_ref[...]
        m_tile = functools.reduce(jnp.maximum, [jnp.max(s, axis=1, keepdims=True) for s in s_list])
        m_new = jnp.maximum(m_prev, m_tile)
        alpha = jnp.exp2(m_prev - m_new)
        p_list = [jnp.exp2(s - m_new[:, 0:1]) for s in s_list]
        l_ref[...] = alpha * l_ref[...] + sum(jnp.sum(p, axis=1, keepdims=True) for p in p_list)
        pv = sum(jnp.dot(p.astype(BF16), v(), preferred_element_type=F32)
                 for p, v in zip(p_list, v_list))
        acc_ref[...] = alpha * acc_ref[...] + pv
        m_ref[...] = m_new

    qm = qm_ref[...]

    def scores(keys, bias_ref):
        return lax.dot_general(qm, keys, nt, preferred_element_type=F32) + bias_ref[...]

    def page_update(lo, hi, last_bias_ref):
        biases = [mask_ref] * (hi - lo - 1) + [last_bias_ref]
        update([scores(kc[...].reshape(flat, QK_DIM).astype(BF16), b)
                for kc, b in zip(kc_refs[lo:hi], biases)],
               [lambda vc=vc: vc[...].reshape(flat, V_DIM).astype(BF16) for vc in vc_refs[lo:hi]])

    starts = range(0, group, pages_per_update)

    @pl.when(jnp.logical_not(last))
    def _far_pages():
        for lo in starts:
            page_update(lo, lo + pages_per_update, mask_ref)

    @pl.when(last)
    def _last_pages():
        for lo in starts:
            page_update(lo, lo + pages_per_update,
                        lastb_ref if lo + pages_per_update == group else mask_ref)
        update([scores(kn_ref[...], newb_ref)], [lambda: vn_ref[...]])
        o = acc_ref[...] / l_ref[...]
        o2 = pltpu.roll(o, rows - dec_seq, axis=0)
        od = o - _diff_lambda(lam_ref, lam_init) * o2
        o_ref[...] = _head_norm(od, g_ref[...], lam_init)


N_PROMPT_SCRATCH = 12
N_DECODE_SCRATCH = 7


def _attn_kernel(pt_ref, relb_ref, lam_ref, gcol_ref, grow_ref, qt_ref, k_ref, vt_ref,
                 qs_ref, kn_ref, vn_ref, *rest, tq, tk, lam_init, dec_seq, pages, steps_per_row):
    del pt_ref
    kc_refs, vc_refs = rest[:pages], rest[pages:2 * pages]
    op_ref, os_ref = rest[2 * pages:2 * pages + 2]
    scratch = rest[2 * pages + 2:]
    prompt_scratch, decode_scratch = scratch[:N_PROMPT_SCRATCH], scratch[N_PROMPT_SCRATCH:]
    head, qi = pl.program_id(0), pl.program_id(1)
    step = head * pl.num_programs(1) + qi
    sub = step % steps_per_row
    _decode_pages(relb_ref, lam_ref, grow_ref, qs_ref, kn_ref, vn_ref, kc_refs, vc_refs, os_ref,
                  *decode_scratch, dec_seq=dec_seq, lam_init=lam_init,
                  pages_per_update=SAMPLE_PAGES_PER_UPDATE, build_tables=step == 0,
                  first=sub == 0, last=sub == steps_per_row - 1)
    _prompt_tile(relb_ref, lam_ref, gcol_ref, qt_ref, k_ref, vt_ref, op_ref, *prompt_scratch,
                 tq=tq, tk=tk, lam_init=lam_init, head=head, qi=qi)


def _attn_call(qt, kb, vt, q_rows, k_new, v_new, cache_k, cache_v, page_table, rel_bias, lam_vecs,
               subln_g, lam_init, dec_seq, tq, tk):
    b, _, _, s = qt.shape
    nb, rows, _ = q_rows.shape
    n_pages = page_table.shape[1]
    n_steps = N_HEADS * (s // tq)
    assert s % tq == 0 and tq % tk == 0 and tk >= FAR_DISTANCE and PAGE_SIZE >= FAR_DISTANCE + dec_seq
    assert (nb * n_pages) % n_steps == 0
    pages = nb * n_pages // n_steps
    assert n_pages % pages == 0 and pages % SAMPLE_PAGES_PER_UPDATE == 0
    steps_per_row = n_pages // pages
    flat = PAGE_SIZE * N_HEADS
    n_q = s // tq

    def row_of(h, qi):
        step = h * n_q + qi
        return step // steps_per_row, (step % steps_per_row) * pages

    const = lambda shape: pl.BlockSpec(shape, lambda h, qi, pt: (0,) * len(shape),
                                       pipeline_mode=pl.Buffered(1))
    per_row = lambda shape: pl.BlockSpec((None,) + shape, lambda h, qi, pt: (row_of(h, qi)[0], 0, 0))
    paged = lambda g: pl.BlockSpec(
        (None, PAGE_SIZE, N_HEADS, QK_DIM),
        lambda h, qi, pt: (pt[row_of(h, qi)[0], row_of(h, qi)[1] + g], 0, 0, 0))
    out_tile = pl.BlockSpec((b, None, tq, V_DIM), lambda h, qi, pt: (0, h, qi, 0))
    q_tile = pl.BlockSpec((b, None, QK_DIM, tq), lambda h, qi, pt: (0, h, 0, qi))
    k_seq = pl.BlockSpec((b, None, s, QK_DIM), lambda h, qi, pt: (0, h, 0, 0), pipeline_mode=pl.Buffered(1))
    v_seq = pl.BlockSpec((b, None, V_EXT, s), lambda h, qi, pt: (0, h, 0, 0), pipeline_mode=pl.Buffered(1))
    stat = pltpu.VMEM((b, 1, 2 * tq), F32)
    prompt_scratch = [pltpu.VMEM((b, QK_DIM, 2 * tq), BF16), stat,
                      pltpu.VMEM((b, V_EXT, 2 * tq), F32),
                      pltpu.VMEM((tq // tk + 2, tk, tq), F32),
                      pltpu.VMEM((b, tk, 2 * tq), F32), pltpu.VMEM((b, tk, 2 * tq), F32),
                      pltpu.VMEM((b, tk, 2 * tq), BF16), pltpu.VMEM((b, tk, 2 * tq), BF16),
                      stat, stat, stat, stat]
    decode_scratch = [pltpu.VMEM((rows, V7X_LANES), F32),
                      pltpu.VMEM((rows, V7X_LANES), F32),
                      pltpu.VMEM((rows, V_DIM), F32),
                      pltpu.VMEM((rows, QK_DIM), BF16),
                      pltpu.VMEM((rows, flat), F32),
                      pltpu.VMEM((rows, flat), F32),
                      pltpu.VMEM((rows, k_new.shape[1]), F32)]
    assert len(prompt_scratch) == N_PROMPT_SCRATCH and len(decode_scratch) == N_DECODE_SCRATCH
    grid_spec = pltpu.PrefetchScalarGridSpec(
        num_scalar_prefetch=1,
        grid=(N_HEADS, n_q),
        in_specs=[pl.BlockSpec(memory_space=pltpu.SMEM), const((4, HEAD_DIM)), const((V_DIM, 1)),
                  const((1, V_DIM)), q_tile, k_seq, v_seq,
                  per_row((rows, QK_DIM)), per_row(k_new.shape[1:]), per_row(v_new.shape[1:])]
        + [paged(g) for g in range(pages)] * 2,
        out_specs=[out_tile, per_row((rows, V_DIM))],
        scratch_shapes=prompt_scratch + decode_scratch,
    )
    return pl.pallas_call(
        functools.partial(_attn_kernel, tq=tq, tk=tk, lam_init=lam_init, dec_seq=dec_seq, pages=pages,
                          steps_per_row=steps_per_row),
        grid_spec=grid_spec,
        out_shape=[jax.ShapeDtypeStruct((b, N_HEADS, s, V_DIM), BF16),
                   jax.ShapeDtypeStruct((nb, rows, V_DIM), F32)],
        compiler_params=pltpu.CompilerParams(dimension_semantics=("arbitrary",) * 2,
                                             vmem_limit_bytes=ATTN_VMEM_LIMIT_BYTES),
        name="attention",
    )(page_table, rel_bias, lam_vecs, subln_g.reshape(V_DIM, 1), subln_g.reshape(1, V_DIM), qt, kb, vt,
      q_rows, k_new, v_new, *([cache_k] * pages), *([cache_v] * pages))


def _ffn_kernel(*refs, d_ff, with_proj):
    if with_proj:
        x_ref, a_ref, wo_ref, gmix_ref, gpre_ref, wgu_ref, wd_ref, gpost_ref, o_ref = refs
        a = jnp.concatenate([a_ref[h] for h in range(N_HEADS)], axis=1)
        x = x_ref[...] + _rms(jnp.dot(a, wo_ref[...], preferred_element_type=F32), gmix_ref[...])
    else:
        x_ref, gpre_ref, wgu_ref, wd_ref, gpost_ref, o_ref = refs
        x = x_ref[...]
    hb = _rms(x, gpre_ref[...]).astype(BF16)
    g = jnp.dot(hb, wgu_ref[:, 0:d_ff], preferred_element_type=F32)
    u = jnp.dot(hb, wgu_ref[:, d_ff:2 * d_ff], preferred_element_type=F32)
    a = (g * (1.0 / (1.0 + jnp.exp(-g))) * u).astype(BF16)
    y = jnp.dot(a, wd_ref[...], preferred_element_type=F32)
    o_ref[...] = x + _rms(y, gpost_ref[...])


def _ffn_call(x, g_pre, wgu_bf16, wd_bf16, g_post, tm, proj=None):
    m = x.shape[0]
    d_ff = wd_bf16.shape[0]
    row = lambda width: pl.BlockSpec((tm, width), lambda i: (i, 0))
    ffn_specs = [_const_spec((1, D_MODEL)), _const_spec(wgu_bf16.shape),
                 _const_spec(wd_bf16.shape), _const_spec((1, D_MODEL))]
    ffn_args = (g_pre, wgu_bf16, wd_bf16, g_post)
    if proj is None:
        specs, args = [row(D_MODEL)] + ffn_specs, (x,) + ffn_args
    else:
        a_bf16, wo_bf16, g_mix = proj
        tiles_per_seq = a_bf16.shape[2] // tm
        heads = pl.BlockSpec((None, N_HEADS, tm, V_DIM),
                             lambda i: (i // tiles_per_seq, 0, i % tiles_per_seq, 0))
        specs = [row(D_MODEL), heads, _const_spec(wo_bf16.shape),
                 _const_spec((1, D_MODEL))] + ffn_specs
        args = (x, a_bf16, wo_bf16, g_mix) + ffn_args
    return pl.pallas_call(
        functools.partial(_ffn_kernel, d_ff=d_ff, with_proj=proj is not None),
        grid=(m // tm,),
        in_specs=specs,
        out_specs=row(D_MODEL),
        out_shape=jax.ShapeDtypeStruct((m, D_MODEL), F32),
        compiler_params=_params(1),
        name="swiglu_ffn",
    )(*args)


def _conv_kernel(*refs, tm, seq):
    carry_mode = seq >= tm
    if carry_mode:
        (x_ref, gpre_ref, win_ref, cw_ref, wout_ref, gpost_ref, st_ref,
         o_ref, tail_ref, carry_ref) = refs
    else:
        (x_ref, gpre_ref, win_ref, cw_ref, wout_ref, gpost_ref, inj1_ref, inj2_ref,
         o_ref, u_ref) = refs
    d = D_MODEL
    x = x_ref[...]
    hb = _rms(x, gpre_ref[...]).astype(BF16)
    gc = jnp.dot(hb, win_ref[:, d:2 * d], preferred_element_type=F32)
    xt = jnp.dot(hb, win_ref[:, 2 * d:3 * d], preferred_element_type=F32)
    u = gc * xt
    row = lax.broadcasted_iota(jnp.int32, (tm, d), 0)
    r1 = pltpu.roll(u, 1, axis=0)
    r2 = pltpu.roll(u, 2, axis=0)
    if carry_mode:
        i = pl.program_id(0)
        tiles_per_seq = seq // tm

        @pl.when(i % tiles_per_seq == 0)
        def _seq_start():
            carry_ref[...] = st_ref[...]

        c = carry_ref[...]
        c1 = c[V7X_SUBLANES - 1:V7X_SUBLANES]
        c2 = c[V7X_SUBLANES - 2:V7X_SUBLANES - 1]
        prev1 = jnp.where(row == 0, c1, r1)
        prev2 = jnp.where(row == 0, c2, jnp.where(row == 1, c1, r2))
        tail = u[tm - V7X_SUBLANES:tm]
        carry_ref[...] = tail
        tail_ref[...] = tail
    else:
        t = row % seq
        prev1 = jnp.where(t >= 1, r1, inj1_ref[...])
        prev2 = jnp.where(t >= 2, r2, inj2_ref[...])
        u_ref[...] = u
    cw = cw_ref[...]
    y = cw[0:1] * prev2 + cw[1:2] * prev1 + cw[2:3] * u
    gb = jnp.dot(hb, win_ref[:, 0:d], preferred_element_type=F32)
    z = jnp.dot((gb * y).astype(BF16), wout_ref[...], preferred_element_type=F32)
    o_ref[...] = x + _rms(z, gpost_ref[...])


def _conv_call(x, g_pre, win_bf16, conv_w, wout_bf16, g_post, state, tm, seq):
    m = x.shape[0]
    nb = m // seq
    row = pl.BlockSpec((tm, D_MODEL), lambda i: (i, 0))
    common = [row, _const_spec((1, D_MODEL)), _const_spec(win_bf16.shape),
              _const_spec((CONV_WIDTH, D_MODEL)), _const_spec(wout_bf16.shape),
              _const_spec((1, D_MODEL))]
    kern = functools.partial(_conv_kernel, tm=tm, seq=seq)
    if seq >= tm:
        assert seq % tm == 0
        n_tiles = m // tm
        tiles_per_seq = seq // tm
        pad = jnp.zeros((nb, V7X_SUBLANES - (CONV_WIDTH - 1), D_MODEL), F32)
        st = jnp.concatenate([pad, state], axis=1)
        st_spec = pl.BlockSpec((None, V7X_SUBLANES, D_MODEL), lambda i: (i // tiles_per_seq, 0, 0))
        tail_spec = pl.BlockSpec((None, V7X_SUBLANES, D_MODEL), lambda i: (i, 0, 0))
        out, tails = pl.pallas_call(
            kern, grid=(n_tiles,),
            in_specs=common + [st_spec],
            out_specs=[row, tail_spec],
            out_shape=[jax.ShapeDtypeStruct((m, D_MODEL), F32),
                       jax.ShapeDtypeStruct((n_tiles, V7X_SUBLANES, D_MODEL), F32)],
            scratch_shapes=[pltpu.VMEM((V7X_SUBLANES, D_MODEL), F32)],
            compiler_params=_params(1),
            name="short_conv_seq",
        )(x, g_pre, win_bf16, conv_w, wout_bf16, g_post, st)
        last = tails.reshape(nb, tiles_per_seq, V7X_SUBLANES, D_MODEL)[:, -1]
        return out, last[:, V7X_SUBLANES - (CONV_WIDTH - 1):]
    assert tm % seq == 0 and m == tm and seq >= CONV_WIDTH - 1
    zeros = lambda n: jnp.zeros((nb, n, D_MODEL), F32)
    inj1 = jnp.concatenate([state[:, 1:2], zeros(seq - 1)], axis=1).reshape(m, D_MODEL)
    inj2 = jnp.concatenate([state[:, 0:2], zeros(seq - 2)], axis=1).reshape(m, D_MODEL)
    out, u = pl.pallas_call(
        kern, grid=(1,),
        in_specs=common + [row, row],
        out_specs=[row, row],
        out_shape=[jax.ShapeDtypeStruct((m, D_MODEL), F32)] * 2,
        compiler_params=_params(1),
        name="short_conv_tok",
    )(x, g_pre, win_bf16, conv_w, wout_bf16, g_post, inj1, inj2)
    return out, u.reshape(nb, seq, D_MODEL)[:, seq - (CONV_WIDTH - 1):]


def kernel(x_prompt, x_sample, cache_k, cache_v, state_conv, page_table, rel_bias, w_qkv, w_o, lambda_q1, lambda_k1, lambda_q2, lambda_k2, subln_g, w_conv_in, conv_w, w_conv_out, norm_mix_pre, norm_mix_post, norm_ffn_pre, norm_ffn_post, w_ffn_gu, w_ffn_down):
    b, s, d = x_prompt.shape
    nb, dec_seq, _ = x_sample.shape
    assert d == D_MODEL and page_table.shape[1] * PAGE_SIZE >= FAR_DISTANCE
    mp, ms = b * s, nb * dec_seq
    tm_p = TOKEN_TILE

    wqkv = w_qkv.astype(BF16)
    wo = w_o.astype(BF16)
    win = w_conv_in.astype(BF16)
    wout = w_conv_out.astype(BF16)
    wgu = w_ffn_gu.astype(BF16)
    wd = w_ffn_down.astype(BF16)
    gain = lambda g, i: g[i].reshape(1, D_MODEL)
    lam_vecs = jnp.stack([lambda_q1, lambda_k1, lambda_q2, lambda_k2])
    lam0 = _lambda_init(0)

    xp = x_prompt.reshape(mp, d)
    xs = x_sample.reshape(ms, d)

    g_pre = gain(norm_mix_pre, 0)
    kf_p, vf_p, kb_p, qt_p, vt_p = _qkv_call(xp, g_pre, wqkv, tm_p, s)
    kf_s, vf_s, kb_s, qt_s, _ = _qkv_call(xs, g_pre, wqkv, ms, ms)

    q4 = qt_s.reshape(N_HEADS, QK_DIM, nb, dec_seq).transpose(2, 0, 3, 1)
    q_rows = jnp.stack([q4, q4], axis=2).reshape(nb, N_HEADS * 2 * dec_seq, QK_DIM)
    pad_tok = PAGE_SIZE // N_HEADS - dec_seq
    flat_new = lambda a: jnp.pad(a.reshape(nb, dec_seq, N_HEADS, QK_DIM),
                                 ((0, 0), (0, pad_tok), (0, 0), (0, 0))).reshape(nb, PAGE_SIZE, QK_DIM)
    on_p, on_rows = _attn_call(qt_p, kb_p, vt_p, q_rows, flat_new(kb_s[0].transpose(1, 0, 2)),
                               flat_new(vf_s.astype(BF16)), cache_k, cache_v, page_table, rel_bias,
                               lam_vecs, subln_g, lam0, dec_seq, ATTN_QUERY_TILE, ATTN_KEY_TILE)
    on_s = on_rows.reshape(nb, N_HEADS, 2, dec_seq, V_DIM)[:, :, 0].transpose(1, 0, 2, 3)
    on_s = on_s.reshape(1, N_HEADS, ms, V_DIM).astype(BF16)

    g_post = gain(norm_mix_post, 0)
    ffn = lambda x, i, tm, proj=None: _ffn_call(x, gain(norm_ffn_pre, i), wgu[i], wd[i],
                                                gain(norm_ffn_post, i), tm, proj)
    xp = ffn(xp, 0, tm_p, (on_p, wo, g_post))
    xs = ffn(xs, 0, ms, (on_s, wo, g_post))

    g_pre, g_post = gain(norm_mix_pre, 1), gain(norm_mix_post, 1)
    zero_state = jnp.zeros((b, CONV_WIDTH - 1, d), F32)
    xp, conv_p = _conv_call(xp, g_pre, win, conv_w, wout, g_post, zero_state, tm_p, s)
    xs, conv_s = _conv_call(xs, g_pre, win, conv_w, wout, g_post, state_conv, ms, dec_seq)
    xp = ffn(xp, 1, tm_p)
    xs = ffn(xs, 1, ms)

    return (xp.reshape(b, s, d), xs.reshape(nb, dec_seq, d),
            kf_p.reshape(b, s, N_HEADS, QK_DIM), vf_p.reshape(b, s, N_HEADS, V_DIM),
            kf_s.reshape(nb, dec_seq, N_HEADS, QK_DIM), vf_s.reshape(nb, dec_seq, N_HEADS, V_DIM),
            conv_p, conv_s)
```

```python
import functools
import math

import jax
import jax.numpy as jnp
from jax import lax
from jax.experimental import pallas as pl
from jax.experimental.pallas import tpu as pltpu

D_MODEL = 1024
N_HEADS = 8
HEAD_DIM = 64
QK_DIM = 2 * HEAD_DIM
V_DIM = 2 * HEAD_DIM
ATTN_WIDTH = N_HEADS * V_DIM
N_BUCKETS = 32
MAX_DISTANCE = 128
CONV_WIDTH = 3
PAGE_SIZE = 128
EPS = 1e-6
NEG_INF = -1e30
SCALE = HEAD_DIM ** -0.5
LOG2E = math.log2(math.e)

V7X_LANES = 128
V7X_BF16_SUBLANES = 16
V_EXT = V_DIM + V7X_BF16_SUBLANES
V7X_SUBLANES = 8
VMEM_LIMIT_BYTES = 56 * 1024 * 1024
SAMPLE_PAGES_PER_UPDATE = 8
ATTN_VMEM_LIMIT_BYTES = 60 * 1024 * 1024
TOKEN_TILE = 512
ATTN_QUERY_TILE = 512
ATTN_KEY_TILE = 256
FAR_TRIP_PAIRS = (2, 1)
F32 = jnp.float32
BF16 = jnp.bfloat16


def _lambda_init(layer):
    return 0.8 - 0.6 * math.exp(-0.3 * layer)


def _bucket_uppers():
    max_exact = N_BUCKETS // 2
    n_far = None
    buckets = []
    for n in range(0, 4 * MAX_DISTANCE):
        if n < max_exact:
            b = n
        else:
            r = math.log(n / max_exact) / math.log(MAX_DISTANCE / max_exact) * (N_BUCKETS - max_exact)
            frac = r - math.floor(r)
            assert min(frac, 1 - frac) > 1e-4 or n == max_exact or n >= MAX_DISTANCE, n
            b = min(max_exact + int(r + (1e-6 if n == max_exact else 0.0)), N_BUCKETS - 1)
            if n >= MAX_DISTANCE:
                b = N_BUCKETS - 1
        buckets.append(b)
    uppers = []
    for b in range(N_BUCKETS - 1):
        uppers.append(next(n for n, bb in enumerate(buckets) if bb > b))
    n_far = uppers[-1]
    return tuple(uppers), n_far


BUCKET_UPPERS, FAR_DISTANCE = _bucket_uppers()


def _shifted_bias(n, relb_ref, head):
    far = relb_ref[N_BUCKETS - 1, head]
    val = jnp.zeros(n.shape, F32)
    for b in range(N_BUCKETS - 2, -1, -1):
        val = jnp.where(n < BUCKET_UPPERS[b], (relb_ref[b, head] - far) * LOG2E, val)
    return val


def _diff_lambda(lam_ref, lam_init):
    lv = lam_ref[...]
    a = jnp.sum(lv[0:1] * lv[1:2], axis=1, keepdims=True)
    b = jnp.sum(lv[2:3] * lv[3:4], axis=1, keepdims=True)
    return jnp.exp(a) - jnp.exp(b) + lam_init


def _rms(x, g):
    return x * lax.rsqrt(jnp.mean(x * x, axis=-1, keepdims=True) + EPS) * g


def _head_norm(o, g, lam_init):
    return o * lax.rsqrt(jnp.mean(o * o, axis=-1, keepdims=True) + EPS) * g * (1.0 - lam_init)


def _const_spec(shape):
    return pl.BlockSpec(shape, lambda *_: (0,) * len(shape), pipeline_mode=pl.Buffered(1))


def _params(n_grid):
    return pltpu.CompilerParams(dimension_semantics=("arbitrary",) * n_grid,
                                vmem_limit_bytes=VMEM_LIMIT_BYTES)


def _qkv_kernel(x_ref, g_ref, w_ref, kf_ref, vf_ref, kb_ref, qt_ref, vt_ref):
    hb = _rms(x_ref[...], g_ref[...]).astype(BF16)
    w = ATTN_WIDTH
    tm = hb.shape[0]
    q = jnp.dot(hb, w_ref[:, 0:w], preferred_element_type=F32)
    qt_ref[...] = (q * (SCALE * LOG2E)).T.astype(BF16).reshape(N_HEADS, QK_DIM, tm)
    k = jnp.dot(hb, w_ref[:, w:2 * w], preferred_element_type=F32)
    kf_ref[...] = k
    for h in range(N_HEADS):
        kb_ref[h] = k[:, h * QK_DIM:(h + 1) * QK_DIM].astype(BF16)
    v = jnp.dot(hb, w_ref[:, 2 * w:3 * w], preferred_element_type=F32)
    vf_ref[...] = v
    vt_ref[:, 0:V_DIM, :] = v.T.astype(BF16).reshape(N_HEADS, V_DIM, tm)
    vt_ref[:, V_DIM:V_EXT, :] = jnp.ones((N_HEADS, V_EXT - V_DIM, tm), BF16)


def _qkv_call(x, g, w_bf16, tm, seq):
    m = x.shape[0]
    tiles_per_seq = seq // tm
    row = lambda width: pl.BlockSpec((tm, width), lambda i: (i, 0))
    fmaj = lambda rows: pl.BlockSpec((None, N_HEADS, rows, tm),
                                     lambda i: (i // tiles_per_seq, 0, 0, i % tiles_per_seq))
    hmaj = pl.BlockSpec((None, N_HEADS, tm, QK_DIM),
                        lambda i: (i // tiles_per_seq, 0, i % tiles_per_seq, 0))
    return pl.pallas_call(
        _qkv_kernel,
        grid=(m // tm,),
        in_specs=[row(D_MODEL), _const_spec((1, D_MODEL)), _const_spec(w_bf16.shape)],
        out_specs=[row(ATTN_WIDTH)] * 2 + [hmaj, fmaj(QK_DIM), fmaj(V_EXT)],
        out_shape=[jax.ShapeDtypeStruct((m, ATTN_WIDTH), F32)] * 2
        + [jax.ShapeDtypeStruct((m // seq, N_HEADS, seq, QK_DIM), BF16),
           jax.ShapeDtypeStruct((m // seq, N_HEADS, QK_DIM, seq), BF16),
           jax.ShapeDtypeStruct((m // seq, N_HEADS, V_EXT, seq), BF16)],
        compiler_params=_params(1),
        name="qkv_proj",
    )(x, g, w_bf16)


def _prompt_tile(relb_ref, lam_ref, g_ref, qt_ref, k_ref, vt_ref, o_ref,
                 q2_ref, m_ref, acc_ref, bias_ref, s0_ref, s1_ref, p0_ref, p1_ref,
                 a0_ref, a1_ref, t0_ref, t1_ref, *, tq, tk, lam_init, head, qi, before_drain):
    chains = range(qt_ref.shape[0])
    ratio = tq // tk
    n_band = ratio + 1
    last = ratio * (qi + 1) - 1

    @pl.when(qi == 0)
    def _build_bias():
        j = lax.broadcasted_iota(jnp.int32, (tk, tq), 0)
        i = lax.broadcasted_iota(jnp.int32, (tk, tq), 1)
        for d in range(n_band):
            n = i - j + tk * (d + 1 - ratio)
            bias_ref[d] = jnp.where(n >= 0, _shifted_bias(n, relb_ref, head), NEG_INF)
        bias_ref[n_band] = jnp.zeros((tk, tq), F32)

    feat = lax.broadcasted_iota(jnp.int32, (QK_DIM, tq), 0)
    for c in chains:
        qt = qt_ref[c]
        zero = jnp.zeros_like(qt)
        q2_ref[c, :, 0:tq] = jnp.where(feat < HEAD_DIM, qt, zero)
        q2_ref[c, :, tq:2 * tq] = jnp.where(feat >= HEAD_DIM, qt, zero)
    m_ref[...] = jnp.full(m_ref.shape, NEG_INF, F32)
    acc_ref[...] = jnp.zeros(acc_ref.shape, F32)
    for p_buf, a_buf in ((p0_ref, a0_ref), (p1_ref, a1_ref)):
        p_buf[...] = jnp.zeros(p_buf.shape, BF16)
        a_buf[...] = jnp.ones(a_buf.shape, F32)

    @pl.when((head == 0) & (qi == 0))
    def _first_touch():
        for s_buf, t_buf in ((s0_ref, t0_ref), (s1_ref, t1_ref)):
            s_buf[...] = jnp.zeros(s_buf.shape, F32)
            t_buf[...] = jnp.zeros(t_buf.shape, F32)

    def time_step(tau, parity, *, scores=True, softmax=True, with_bias=False):
        s_buf, t_buf = (s0_ref, t0_ref) if parity == 0 else (s1_ref, t1_ref)
        p_buf, a_buf = (p0_ref, a0_ref) if parity == 0 else (p1_ref, a1_ref)
        jv = pl.multiple_of(jnp.clip(tau - 4, 0, last) * tk, tk)
        valid = (tau >= 2) & (tau - 2 <= last)
        jq = jnp.minimum(tau, last)
        start = pl.multiple_of(jq * tk, tk)
        if scores and with_bias:
            b = bias_ref[jnp.minimum(last - jq, n_band)]
            b2 = jnp.concatenate([b, b], axis=1)

        def values_stage(c):
            pv = jnp.dot(vt_ref[c, :, pl.ds(jv, tk)], p_buf[c], preferred_element_type=F32)
            acc_ref[c] = a_buf[c] * acc_ref[c] + pv

        def softmax_stage(c):
            m_prev = m_ref[c]
            m_new = jnp.where(valid, jnp.maximum(m_prev, t_buf[c]), m_prev)
            a_buf[c] = jnp.exp2(m_prev - m_new)
            p_buf[c] = jnp.exp2((s_buf[c] - jnp.where(valid, m_new, -NEG_INF)).astype(BF16))
            m_ref[c] = m_new

        def scores_stage(c):
            s_new = jnp.dot(k_ref[c, pl.ds(start, tk), :], q2_ref[c],
                            preferred_element_type=F32)
            if with_bias:
                s_new = s_new + b2
            s_buf[c] = s_new
            t_buf[c] = jnp.max(s_new, axis=0, keepdims=True)

        for c in chains:
            values_stage(c)
            if softmax:
                softmax_stage(c)
            if scores:
                scores_stage(c)

    n_far = jnp.maximum(last + 1 - n_band, 0)
    n_pairs = n_far // 2
    done = 0
    for pairs_per_trip in FAR_TRIP_PAIRS:
        def far_trip(u, carry, pairs_per_trip=pairs_per_trip):
            for d in range(2 * pairs_per_trip):
                time_step(2 * u * pairs_per_trip + d, d % 2)
            return carry

        trips = n_pairs // pairs_per_trip
        lax.fori_loop(done // pairs_per_trip, trips, far_trip, 0)
        done = trips * pairs_per_trip
    tau0 = 2 * n_pairs
    n_tail = n_band + 1
    before_drain()
    for d in range(n_tail + 4):
        time_step(tau0 + d, d % 2, scores=d < n_tail, softmax=d < n_tail + 2, with_bias=True)

    lam = _diff_lambda(lam_ref, lam_init)
    gain = g_ref[...] * (1.0 - lam_init)
    for c in chains:
        inv_l = 1.0 / acc_ref[c, V_DIM:V_DIM + 1, :]
        o1 = acc_ref[c, 0:V_DIM, 0:tq] * inv_l[:, 0:tq]
        o2 = acc_ref[c, 0:V_DIM, tq:2 * tq] * inv_l[:, tq:2 * tq]
        ot = o1 - lam * o2
        ms = jnp.mean(ot * ot, axis=0, keepdims=True)
        o_ref[c] = (ot * lax.rsqrt(ms + EPS) * gain).T.astype(BF16)


def _decode_stages(relb_ref, lam_ref, g_ref, q_ref, kn_ref, vn_ref, kc_refs, vc_refs, o_ref,
                   m_ref, l_ref, acc_ref, qm_ref, pageb_ref, newb_ref,
                   *, dec_seq, lam_init, pages_per_update):
    group = len(kc_refs)
    rows = N_HEADS * 2 * dec_seq
    per_head = 2 * dec_seq
    assert per_head == V7X_SUBLANES
    flat = PAGE_SIZE * N_HEADS
    nt = (((1,), (1,)), ((), ()))

    def grids(width):
        r = lax.broadcasted_iota(jnp.int32, (rows, width), 0)
        c = lax.broadcasted_iota(jnp.int32, (rows, width), 1)
        same_head = (c % N_HEADS) == (r // per_head)
        return r % dec_seq, c // N_HEADS, same_head

    def head_bias(dist):
        parts = [_shifted_bias(dist[h * per_head:(h + 1) * per_head], relb_ref, h)
                 for h in range(N_HEADS)]
        return jnp.concatenate(parts, axis=0)

    def begin(build_tables, first):
        @pl.when(build_tables)
        def _tables():
            tq, tk, ok = grids(flat)
            pageb_ref[0] = jnp.where(ok, 0.0, NEG_INF)
            pageb_ref[1] = jnp.where(ok, head_bias(PAGE_SIZE + tq - tk), NEG_INF)
            tqn, tkn, okn = grids(kn_ref.shape[0])
            newb_ref[...] = jnp.where(okn & (tkn <= tqn), head_bias(tqn - tkn), NEG_INF)

        @pl.when(first)
        def _init():
            m_ref[...] = jnp.full(m_ref.shape, NEG_INF, F32)
            l_ref[...] = jnp.zeros(l_ref.shape, F32)
            acc_ref[...] = jnp.zeros(acc_ref.shape, F32)
            q = q_ref[...]
            row = lax.broadcasted_iota(jnp.int32, (rows, QK_DIM), 0)
            lane = lax.broadcasted_iota(jnp.int32, (rows, QK_DIM), 1)
            first_comp = (row // dec_seq) % 2 == 0
            qm_ref[...] = jnp.where((lane < HEAD_DIM) == first_comp, q, jnp.zeros_like(q))

    def update(s_list, v_list):
        m_prev = m_ref[...]
        m_tile = functools.reduce(jnp.maximum, [jnp.max(s, axis=1, keepdims=True) for s in s_list])
        m_new = jnp.maximum(m_prev, m_tile)
        alpha = jnp.exp2(m_prev - m_new)
        p_list = [jnp.exp2(s - m_new[:, 0:1]) for s in s_list]
        l_ref[...] = alpha * l_ref[...] + sum(jnp.sum(p, axis=1, keepdims=True) for p in p_list)
        pv = sum(jnp.dot(p.astype(BF16), v(), preferred_element_type=F32)
                 for p, v in zip(p_list, v_list))
        acc_ref[...] = alpha * acc_ref[...] + pv
        m_ref[...] = m_new

    def scores(keys, bias):
        return lax.dot_general(qm_ref[...], keys, nt, preferred_element_type=F32) + bias

    def pages(last):
        for lo in range(0, group, pages_per_update):
            hi = lo + pages_per_update
            slots = [0] * (hi - lo - 1) + [last.astype(jnp.int32) if hi == group else 0]
            update([scores(kc[...].reshape(flat, QK_DIM).astype(BF16), pageb_ref[slot])
                    for kc, slot in zip(kc_refs[lo:hi], slots)],
                   [lambda vc=vc: vc[...].reshape(flat, V_DIM).astype(BF16) for vc in vc_refs[lo:hi]])

    def end(last):
        @pl.when(last)
        def _finish():
            update([scores(kn_ref[...], newb_ref[...])], [lambda: vn_ref[...]])
            o = acc_ref[...] / l_ref[...]
            o2 = pltpu.roll(o, rows - dec_seq, axis=0)
            od = o - _diff_lambda(lam_ref, lam_init) * o2
            o_ref[...] = _head_norm(od, g_ref[...], lam_init)

    return begin, pages, end


N_PROMPT_SCRATCH = 12
N_DECODE_SCRATCH = 6


def _attn_kernel(pt_ref, relb_ref, lam_ref, gcol_ref, grow_ref, qt_ref, k_ref, vt_ref,
                 qs_ref, kn_ref, vn_ref, *rest, tq, tk, lam_init, dec_seq, pages, steps_per_row):
    del pt_ref
    kc_refs, vc_refs = rest[:pages], rest[pages:2 * pages]
    op_ref, os_ref = rest[2 * pages:2 * pages + 2]
    scratch = rest[2 * pages + 2:]
    prompt_scratch, decode_scratch = scratch[:N_PROMPT_SCRATCH], scratch[N_PROMPT_SCRATCH:]
    head, qi = pl.program_id(0), pl.program_id(1)
    step = head * pl.num_programs(1) + qi
    sub = step % steps_per_row
    last = sub == steps_per_row - 1
    decode_begin, decode_pages, decode_end = _decode_stages(
        relb_ref, lam_ref, grow_ref, qs_ref, kn_ref, vn_ref, kc_refs, vc_refs, os_ref, *decode_scratch,
        dec_seq=dec_seq, lam_init=lam_init, pages_per_update=SAMPLE_PAGES_PER_UPDATE)
    decode_begin(step == 0, sub == 0)
    _prompt_tile(relb_ref, lam_ref, gcol_ref, qt_ref, k_ref, vt_ref, op_ref, *prompt_scratch,
                 tq=tq, tk=tk, lam_init=lam_init, head=head, qi=qi,
                 before_drain=lambda: decode_pages(last))
    decode_end(last)


def _attn_call(qt, kb, vt, q_rows, k_new, v_new, cache_k, cache_v, page_table, rel_bias, lam_vecs,
               subln_g, lam_init, dec_seq, tq, tk):
    b, _, _, s = qt.shape
    nb, rows, _ = q_rows.shape
    n_pages = page_table.shape[1]
    n_steps = N_HEADS * (s // tq)
    assert s % tq == 0 and tq % tk == 0 and tk >= FAR_DISTANCE and PAGE_SIZE >= FAR_DISTANCE + dec_seq
    assert (nb * n_pages) % n_steps == 0
    pages = nb * n_pages // n_steps
    assert n_pages % pages == 0 and pages % SAMPLE_PAGES_PER_UPDATE == 0
    steps_per_row = n_pages // pages
    flat = PAGE_SIZE * N_HEADS
    n_q = s // tq

    def row_of(h, qi):
        step = h * n_q + qi
        return step // steps_per_row, (step % steps_per_row) * pages

    const = lambda shape: pl.BlockSpec(shape, lambda h, qi, pt: (0,) * len(shape),
                                       pipeline_mode=pl.Buffered(1))
    per_row = lambda shape: pl.BlockSpec((None,) + shape, lambda h, qi, pt: (row_of(h, qi)[0], 0, 0))
    paged = lambda g: pl.BlockSpec(
        (None, PAGE_SIZE, N_HEADS, QK_DIM),
        lambda h, qi, pt: (pt[row_of(h, qi)[0], row_of(h, qi)[1] + g], 0, 0, 0))
    out_tile = pl.BlockSpec((b, None, tq, V_DIM), lambda h, qi, pt: (0, h, qi, 0))
    q_tile = pl.BlockSpec((b, None, QK_DIM, tq), lambda h, qi, pt: (0, h, 0, qi))
    k_seq = pl.BlockSpec((b, None, s, QK_DIM), lambda h, qi, pt: (0, h, 0, 0), pipeline_mode=pl.Buffered(1))
    v_seq = pl.BlockSpec((b, None, V_EXT, s), lambda h, qi, pt: (0, h, 0, 0), pipeline_mode=pl.Buffered(1))
    stat = pltpu.VMEM((b, 1, 2 * tq), F32)
    prompt_scratch = [pltpu.VMEM((b, QK_DIM, 2 * tq), BF16), stat,
                      pltpu.VMEM((b, V_EXT, 2 * tq), F32),
                      pltpu.VMEM((tq // tk + 2, tk, tq), F32),
                      pltpu.VMEM((b, tk, 2 * tq), F32), pltpu.VMEM((b, tk, 2 * tq), F32),
                      pltpu.VMEM((b, tk, 2 * tq), BF16), pltpu.VMEM((b, tk, 2 * tq), BF16),
                      stat, stat, stat, stat]
    decode_scratch = [pltpu.VMEM((rows, V7X_LANES), F32),
                      pltpu.VMEM((rows, V7X_LANES), F32),
                      pltpu.VMEM((rows, V_DIM), F32),
                      pltpu.VMEM((rows, QK_DIM), BF16),
                      pltpu.VMEM((2, rows, flat), F32),
                      pltpu.VMEM((rows, k_new.shape[1]), F32)]
    assert len(prompt_scratch) == N_PROMPT_SCRATCH and len(decode_scratch) == N_DECODE_SCRATCH
    grid_spec = pltpu.PrefetchScalarGridSpec(
        num_scalar_prefetch=1,
        grid=(N_HEADS, n_q),
        in_specs=[pl.BlockSpec(memory_space=pltpu.SMEM), const((4, HEAD_DIM)), const((V_DIM, 1)),
                  const((1, V_DIM)), q_tile, k_seq, v_seq,
                  per_row((rows, QK_DIM)), per_row(k_new.shape[1:]), per_row(v_new.shape[1:])]
        + [paged(g) for g in range(pages)] * 2,
        out_specs=[out_tile, per_row((rows, V_DIM))],
        scratch_shapes=prompt_scratch + decode_scratch,
    )
    return pl.pallas_call(
        functools.partial(_attn_kernel, tq=tq, tk=tk, lam_init=lam_init, dec_seq=dec_seq, pages=pages,
                          steps_per_row=steps_per_row),
        grid_spec=grid_spec,
        out_shape=[jax.ShapeDtypeStruct((b, N_HEADS, s, V_DIM), BF16),
                   jax.ShapeDtypeStruct((nb, rows, V_DIM), F32)],
        compiler_params=pltpu.CompilerParams(dimension_semantics=("arbitrary",) * 2,
                                             vmem_limit_bytes=ATTN_VMEM_LIMIT_BYTES),
        name="attention",
    )(page_table, rel_bias, lam_vecs, subln_g.reshape(V_DIM, 1), subln_g.reshape(1, V_DIM), qt, kb, vt,
      q_rows, k_new, v_new, *([cache_k] * pages), *([cache_v] * pages))


def _ffn_kernel(*refs, d_ff, with_proj):
    if with_proj:
        x_ref, a_ref, wo_ref, gmix_ref, gpre_ref, wgu_ref, wd_ref, gpost_ref, o_ref = refs
        a = jnp.concatenate([a_ref[h] for h in range(N_HEADS)], axis=1)
        x = x_ref[...] + _rms(jnp.dot(a, wo_ref[...], preferred_element_type=F32), gmix_ref[...])
    else:
        x_ref, gpre_ref, wgu_ref, wd_ref, gpost_ref, o_ref = refs
        x = x_ref[...]
    hb = _rms(x, gpre_ref[...]).astype(BF16)
    g = jnp.dot(hb, wgu_ref[:, 0:d_ff], preferred_element_type=F32)
    u = jnp.dot(hb, wgu_ref[:, d_ff:2 * d_ff], preferred_element_type=F32)
    a = (g * (1.0 / (1.0 + jnp.exp(-g))) * u).astype(BF16)
    y = jnp.dot(a, wd_ref[...], preferred_element_type=F32)
    o_ref[...] = x + _rms(y, gpost_ref[...])


def _ffn_call(x, g_pre, wgu_bf16, wd_bf16, g_post, tm, proj=None):
    m = x.shape[0]
    d_ff = wd_bf16.shape[0]
    row = lambda width: pl.BlockSpec((tm, width), lambda i: (i, 0))
    ffn_specs = [_const_spec((1, D_MODEL)), _const_spec(wgu_bf16.shape),
                 _const_spec(wd_bf16.shape), _const_spec((1, D_MODEL))]
    ffn_args = (g_pre, wgu_bf16, wd_bf16, g_post)
    if proj is None:
        specs, args = [row(D_MODEL)] + ffn_specs, (x,) + ffn_args
    else:
        a_bf16, wo_bf16, g_mix = proj
        tiles_per_seq = a_bf16.shape[2] // tm
        heads = pl.BlockSpec((None, N_HEADS, tm, V_DIM),
                             lambda i: (i // tiles_per_seq, 0, i % tiles_per_seq, 0))
        specs = [row(D_MODEL), heads, _const_spec(wo_bf16.shape),
                 _const_spec((1, D_MODEL))] + ffn_specs
        args = (x, a_bf16, wo_bf16, g_mix) + ffn_args
    return pl.pallas_call(
        functools.partial(_ffn_kernel, d_ff=d_ff, with_proj=proj is not None),
        grid=(m // tm,),
        in_specs=specs,
        out_specs=row(D_MODEL),
        out_shape=jax.ShapeDtypeStruct((m, D_MODEL), F32),
        compiler_params=_params(1),
        name="swiglu_ffn",
    )(*args)


def _conv_kernel(*refs, tm, seq):
    carry_mode = seq >= tm
    if carry_mode:
        (x_ref, gpre_ref, win_ref, cw_ref, wout_ref, gpost_ref, st_ref,
         o_ref, tail_ref, carry_ref) = refs
    else:
        (x_ref, gpre_ref, win_ref, cw_ref, wout_ref, gpost_ref, inj1_ref, inj2_ref,
         o_ref, u_ref) = refs
    d = D_MODEL
    x = x_ref[...]
    hb = _rms(x, gpre_ref[...]).astype(BF16)
    gc = jnp.dot(hb, win_ref[:, d:2 * d], preferred_element_type=F32)
    xt = jnp.dot(hb, win_ref[:, 2 * d:3 * d], preferred_element_type=F32)
    u = gc * xt
    row = lax.broadcasted_iota(jnp.int32, (tm, d), 0)
    r1 = pltpu.roll(u, 1, axis=0)
    r2 = pltpu.roll(u, 2, axis=0)
    if carry_mode:
        i = pl.program_id(0)
        tiles_per_seq = seq // tm

        @pl.when(i % tiles_per_seq == 0)
        def _seq_start():
            carry_ref[...] = st_ref[...]

        c = carry_ref[...]
        c1 = c[V7X_SUBLANES - 1:V7X_SUBLANES]
        c2 = c[V7X_SUBLANES - 2:V7X_SUBLANES - 1]
        prev1 = jnp.where(row == 0, c1, r1)
        prev2 = jnp.where(row == 0, c2, jnp.where(row == 1, c1, r2))
        tail = u[tm - V7X_SUBLANES:tm]
        carry_ref[...] = tail
        tail_ref[...] = tail
    else:
        t = row % seq
        prev1 = jnp.where(t >= 1, r1, inj1_ref[...])
        prev2 = jnp.where(t >= 2, r2, inj2_ref[...])
        u_ref[...] = u
    cw = cw_ref[...]
    y = cw[0:1] * prev2 + cw[1:2] * prev1 + cw[2:3] * u
    gb = jnp.dot(hb, win_ref[:, 0:d], preferred_element_type=F32)
    z = jnp.dot((gb * y).astype(BF16), wout_ref[...], preferred_element_type=F32)
    o_ref[...] = x + _rms(z, gpost_ref[...])


def _conv_call(x, g_pre, win_bf16, conv_w, wout_bf16, g_post, state, tm, seq):
    m = x.shape[0]
    nb = m // seq
    row = pl.BlockSpec((tm, D_MODEL), lambda i: (i, 0))
    common = [row, _const_spec((1, D_MODEL)), _const_spec(win_bf16.shape),
              _const_spec((CONV_WIDTH, D_MODEL)), _const_spec(wout_bf16.shape),
              _const_spec((1, D_MODEL))]
    kern = functools.partial(_conv_kernel, tm=tm, seq=seq)
    if seq >= tm:
        assert seq % tm == 0
        n_tiles = m // tm
        tiles_per_seq = seq // tm
        pad = jnp.zeros((nb, V7X_SUBLANES - (CONV_WIDTH - 1), D_MODEL), F32)
        st = jnp.concatenate([pad, state], axis=1)
        st_spec = pl.BlockSpec((None, V7X_SUBLANES, D_MODEL), lambda i: (i // tiles_per_seq, 0, 0))
        tail_spec = pl.BlockSpec((None, V7X_SUBLANES, D_MODEL), lambda i: (i, 0, 0))
        out, tails = pl.pallas_call(
            kern, grid=(n_tiles,),
            in_specs=common + [st_spec],
            out_specs=[row, tail_spec],
            out_shape=[jax.ShapeDtypeStruct((m, D_MODEL), F32),
                       jax.ShapeDtypeStruct((n_tiles, V7X_SUBLANES, D_MODEL), F32)],
            scratch_shapes=[pltpu.VMEM((V7X_SUBLANES, D_MODEL), F32)],
            compiler_params=_params(1),
            name="short_conv_seq",
        )(x, g_pre, win_bf16, conv_w, wout_bf16, g_post, st)
        last = tails.reshape(nb, tiles_per_seq, V7X_SUBLANES, D_MODEL)[:, -1]
        return out, last[:, V7X_SUBLANES - (CONV_WIDTH - 1):]
    assert tm % seq == 0 and m == tm and seq >= CONV_WIDTH - 1
    zeros = lambda n: jnp.zeros((nb, n, D_MODEL), F32)
    inj1 = jnp.concatenate([state[:, 1:2], zeros(seq - 1)], axis=1).reshape(m, D_MODEL)
    inj2 = jnp.concatenate([state[:, 0:2], zeros(seq - 2)], axis=1).reshape(m, D_MODEL)
    out, u = pl.pallas_call(
        kern, grid=(1,),
        in_specs=common + [row, row],
        out_specs=[row, row],
        out_shape=[jax.ShapeDtypeStruct((m, D_MODEL), F32)] * 2,
        compiler_params=_params(1),
        name="short_conv_tok",
    )(x, g_pre, win_bf16, conv_w, wout_bf16, g_post, inj1, inj2)
    return out, u.reshape(nb, seq, D_MODEL)[:, seq - (CONV_WIDTH - 1):]


def kernel(x_prompt, x_sample, cache_k, cache_v, state_conv, page_table, rel_bias, w_qkv, w_o, lambda_q1, lambda_k1, lambda_q2, lambda_k2, subln_g, w_conv_in, conv_w, w_conv_out, norm_mix_pre, norm_mix_post, norm_ffn_pre, norm_ffn_post, w_ffn_gu, w_ffn_down):
    b, s, d = x_prompt.shape
    nb, dec_seq, _ = x_sample.shape
    assert d == D_MODEL and page_table.shape[1] * PAGE_SIZE >= FAR_DISTANCE
    mp, ms = b * s, nb * dec_seq
    tm_p = TOKEN_TILE

    wqkv = w_qkv.astype(BF16)
    wo = w_o.astype(BF16)
    win = w_conv_in.astype(BF16)
    wout = w_conv_out.astype(BF16)
    wgu = w_ffn_gu.astype(BF16)
    wd = w_ffn_down.astype(BF16)
    gain = lambda g, i: g[i].reshape(1, D_MODEL)
    lam_vecs = jnp.stack([lambda_q1, lambda_k1, lambda_q2, lambda_k2])
    lam0 = _lambda_init(0)

    xp = x_prompt.reshape(mp, d)
    xs = x_sample.reshape(ms, d)

    g_pre = gain(norm_mix_pre, 0)
    kf_p, vf_p, kb_p, qt_p, vt_p = _qkv_call(xp, g_pre, wqkv, tm_p, s)
    kf_s, vf_s, kb_s, qt_s, _ = _qkv_call(xs, g_pre, wqkv, ms, ms)

    q4 = qt_s.reshape(N_HEADS, QK_DIM, nb, dec_seq).transpose(2, 0, 3, 1)
    q_rows = jnp.stack([q4, q4], axis=2).reshape(nb, N_HEADS * 2 * dec_seq, QK_DIM)
    pad_tok = PAGE_SIZE // N_HEADS - dec_seq
    flat_new = lambda a: jnp.pad(a.reshape(nb, dec_seq, N_HEADS, QK_DIM),
                                 ((0, 0), (0, pad_tok), (0, 0), (0, 0))).reshape(nb, PAGE_SIZE, QK_DIM)
    on_p, on_rows = _attn_call(qt_p, kb_p, vt_p, q_rows, flat_new(kb_s[0].transpose(1, 0, 2)),
                               flat_new(vf_s.astype(BF16)), cache_k, cache_v, page_table, rel_bias,
                               lam_vecs, subln_g, lam0, dec_seq, ATTN_QUERY_TILE, ATTN_KEY_TILE)
    on_s = on_rows.reshape(nb, N_HEADS, 2, dec_seq, V_DIM)[:, :, 0].transpose(1, 0, 2, 3)
    on_s = on_s.reshape(1, N_HEADS, ms, V_DIM).astype(BF16)

    g_post = gain(norm_mix_post, 0)
    ffn = lambda x, i, tm, proj=None: _ffn_call(x, gain(norm_ffn_pre, i), wgu[i], wd[i],
                                                gain(norm_ffn_post, i), tm, proj)
    xp = ffn(xp, 0, tm_p, (on_p, wo, g_post))
    xs = ffn(xs, 0, ms, (on_s, wo, g_post))

    g_pre, g_post = gain(norm_mix_pre, 1), gain(norm_mix_post, 1)
    zero_state = jnp.zeros((b, CONV_WIDTH - 1, d), F32)
    xp, conv_p = _conv_call(xp, g_pre, win, conv_w, wout, g_post, zero_state, tm_p, s)
    xs, conv_s = _conv_call(xs, g_pre, win, conv_w, wout, g_post, state_conv, ms, dec_seq)
    xp = ffn(xp, 1, tm_p)
    xs = ffn(xs, 1, ms)

    return (xp.reshape(b, s, d), xs.reshape(nb, dec_seq, d),
            kf_p.reshape(b, s, N_HEADS, QK_DIM), vf_p.reshape(b, s, N_HEADS, V_DIM),
            kf_s.reshape(nb, dec_seq, N_HEADS, QK_DIM), vf_s.reshape(nb, dec_seq, N_HEADS, V_DIM),
            conv_p, conv_s)
```

```python
import functools
import math

import jax
import jax.numpy as jnp
from jax import lax
from jax.experimental import pallas as pl
from jax.experimental.pallas import tpu as pltpu

D_MODEL = 1024
N_HEADS = 8
HEAD_DIM = 64
QK_DIM = 2 * HEAD_DIM
V_DIM = 2 * HEAD_DIM
ATTN_WIDTH = N_HEADS * V_DIM
N_BUCKETS = 32
MAX_DISTANCE = 128
CONV_WIDTH = 3
PAGE_SIZE = 128
EPS = 1e-6
NEG_INF = -1e30
SCALE = HEAD_DIM ** -0.5
LOG2E = math.log2(math.e)

V7X_LANES = 128
V7X_BF16_SUBLANES = 16
V_EXT = V_DIM + V7X_BF16_SUBLANES
V7X_SUBLANES = 8
VMEM_LIMIT_BYTES = 56 * 1024 * 1024
SAMPLE_PAGES_PER_UPDATE = 8
ATTN_VMEM_LIMIT_BYTES = 60 * 1024 * 1024
TOKEN_TILE = 512
ATTN_QUERY_TILE = 512
ATTN_KEY_TILE = 256
FAR_TRIP_PAIRS = (2, 1)
F32 = jnp.float32
BF16 = jnp.bfloat16


def _lambda_init(layer):
    return 0.8 - 0.6 * math.exp(-0.3 * layer)


def _bucket_uppers():
    max_exact = N_BUCKETS // 2
    n_far = None
    buckets = []
    for n in range(0, 4 * MAX_DISTANCE):
        if n < max_exact:
            b = n
        else:
            r = math.log(n / max_exact) / math.log(MAX_DISTANCE / max_exact) * (N_BUCKETS - max_exact)
            frac = r - math.floor(r)
            assert min(frac, 1 - frac) > 1e-4 or n == max_exact or n >= MAX_DISTANCE, n
            b = min(max_exact + int(r + (1e-6 if n == max_exact else 0.0)), N_BUCKETS - 1)
            if n >= MAX_DISTANCE:
                b = N_BUCKETS - 1
        buckets.append(b)
    uppers = []
    for b in range(N_BUCKETS - 1):
        uppers.append(next(n for n, bb in enumerate(buckets) if bb > b))
    n_far = uppers[-1]
    return tuple(uppers), n_far


BUCKET_UPPERS, FAR_DISTANCE = _bucket_uppers()


def _shifted_bias(n, relb_ref, head):
    far = relb_ref[N_BUCKETS - 1, head]
    val = jnp.zeros(n.shape, F32)
    for b in range(N_BUCKETS - 2, -1, -1):
        val = jnp.where(n < BUCKET_UPPERS[b], (relb_ref[b, head] - far) * LOG2E, val)
    return val


def _diff_lambda(lam_ref, lam_init):
    lv = lam_ref[...]
    a = jnp.sum(lv[0:1] * lv[1:2], axis=1, keepdims=True)
    b = jnp.sum(lv[2:3] * lv[3:4], axis=1, keepdims=True)
    return jnp.exp(a) - jnp.exp(b) + lam_init


def _rms(x, g):
    return x * lax.rsqrt(jnp.mean(x * x, axis=-1, keepdims=True) + EPS) * g


def _head_norm(o, g, lam_init):
    return o * lax.rsqrt(jnp.mean(o * o, axis=-1, keepdims=True) + EPS) * g * (1.0 - lam_init)


def _const_spec(shape):
    return pl.BlockSpec(shape, lambda *_: (0,) * len(shape), pipeline_mode=pl.Buffered(1))


def _params(n_grid):
    return pltpu.CompilerParams(dimension_semantics=("arbitrary",) * n_grid,
                                vmem_limit_bytes=VMEM_LIMIT_BYTES)


def _qkv_kernel(x_ref, g_ref, w_ref, kf_ref, vf_ref, kb_ref, qt_ref, vt_ref):
    hb = _rms(x_ref[...], g_ref[...]).astype(BF16)
    w = ATTN_WIDTH
    tm = hb.shape[0]
    q = jnp.dot(hb, w_ref[:, 0:w], preferred_element_type=F32)
    qt_ref[...] = (q * (SCALE * LOG2E)).T.astype(BF16).reshape(N_HEADS, QK_DIM, tm)
    k = jnp.dot(hb, w_ref[:, w:2 * w], preferred_element_type=F32)
    kf_ref[...] = k
    for h in range(N_HEADS):
        kb_ref[h] = k[:, h * QK_DIM:(h + 1) * QK_DIM].astype(BF16)
    v = jnp.dot(hb, w_ref[:, 2 * w:3 * w], preferred_element_type=F32)
    vf_ref[...] = v
    vt_ref[:, 0:V_DIM, :] = v.T.astype(BF16).reshape(N_HEADS, V_DIM, tm)
    vt_ref[:, V_DIM:V_EXT, :] = jnp.ones((N_HEADS, V_EXT - V_DIM, tm), BF16)


def _qkv_call(x, g, w_bf16, tm, seq):
    m = x.shape[0]
    tiles_per_seq = seq // tm
    row = lambda width: pl.BlockSpec((tm, width), lambda i: (i, 0))
    fmaj = lambda rows: pl.BlockSpec((None, N_HEADS, rows, tm),
                                     lambda i: (i // tiles_per_seq, 0, 0, i % tiles_per_seq))
    hmaj = pl.BlockSpec((None, N_HEADS, tm, QK_DIM),
                        lambda i: (i // tiles_per_seq, 0, i % tiles_per_seq, 0))
    return pl.pallas_call(
        _qkv_kernel,
        grid=(m // tm,),
        in_specs=[row(D_MODEL), _const_spec((1, D_MODEL)), _const_spec(w_bf16.shape)],
        out_specs=[row(ATTN_WIDTH)] * 2 + [hmaj, fmaj(QK_DIM), fmaj(V_EXT)],
        out_shape=[jax.ShapeDtypeStruct((m, ATTN_WIDTH), F32)] * 2
        + [jax.ShapeDtypeStruct((m // seq, N_HEADS, seq, QK_DIM), BF16),
           jax.ShapeDtypeStruct((m // seq, N_HEADS, QK_DIM, seq), BF16),
           jax.ShapeDtypeStruct((m // seq, N_HEADS, V_EXT, seq), BF16)],
        compiler_params=_params(1),
        name="qkv_proj",
    )(x, g, w_bf16)


def _prompt_tile(relb_ref, lam_ref, g_ref, qt_ref, k_ref, vt_ref, o_ref,
                 q2_ref, m_ref, acc_ref, bias_ref, s0_ref, s1_ref, p0_ref, p1_ref,
                 a0_ref, a1_ref, t0_ref, t1_ref, *, tq, tk, lam_init, head, qi, before_drain):
    chains = range(qt_ref.shape[0])
    ratio = tq // tk
    n_band = ratio + 1
    last = ratio * (qi + 1) - 1

    @pl.when(qi == 0)
    def _build_bias():
        j = lax.broadcasted_iota(jnp.int32, (tk, tq), 0)
        i = lax.broadcasted_iota(jnp.int32, (tk, tq), 1)
        for d in range(n_band):
            n = i - j + tk * (d + 1 - ratio)
            bias_ref[d] = jnp.where(n >= 0, _shifted_bias(n, relb_ref, head), NEG_INF)
        bias_ref[n_band] = jnp.zeros((tk, tq), F32)

    feat = lax.broadcasted_iota(jnp.int32, (QK_DIM, tq), 0)
    for c in chains:
        qt = qt_ref[c]
        zero = jnp.zeros_like(qt)
        q2_ref[c, :, 0:tq] = jnp.where(feat < HEAD_DIM, qt, zero)
        q2_ref[c, :, tq:2 * tq] = jnp.where(feat >= HEAD_DIM, qt, zero)
    m_ref[...] = jnp.full(m_ref.shape, NEG_INF, F32)
    acc_ref[...] = jnp.zeros(acc_ref.shape, F32)
    for p_buf, a_buf in ((p0_ref, a0_ref), (p1_ref, a1_ref)):
        p_buf[...] = jnp.zeros(p_buf.shape, BF16)
        a_buf[...] = jnp.ones(a_buf.shape, F32)

    @pl.when((head == 0) & (qi == 0))
    def _first_touch():
        for s_buf, t_buf in ((s0_ref, t0_ref), (s1_ref, t1_ref)):
            s_buf[...] = jnp.zeros(s_buf.shape, F32)
            t_buf[...] = jnp.zeros(t_buf.shape, F32)

    def time_step(tau, parity, *, scores=True, softmax=True, with_bias=False):
        s_buf, t_buf = (s0_ref, t0_ref) if parity == 0 else (s1_ref, t1_ref)
        p_buf, a_buf = (p0_ref, a0_ref) if parity == 0 else (p1_ref, a1_ref)
        jv = pl.multiple_of(jnp.clip(tau - 4, 0, last) * tk, tk)
        valid = (tau >= 2) & (tau - 2 <= last)
        jq = jnp.minimum(tau, last)
        start = pl.multiple_of(jq * tk, tk)
        if scores and with_bias:
            b = bias_ref[jnp.minimum(last - jq, n_band)]
            b2 = jnp.concatenate([b, b], axis=1)

        def values_stage(c):
            pv = jnp.dot(vt_ref[c, :, pl.ds(jv, tk)], p_buf[c], preferred_element_type=F32)
            acc_ref[c] = a_buf[c] * acc_ref[c] + pv

        def softmax_stage(c):
            m_prev = m_ref[c]
            m_new = jnp.where(valid, jnp.maximum(m_prev, t_buf[c]), m_prev)
            a_buf[c] = jnp.exp2(m_prev - m_new)
            p_buf[c] = jnp.exp2((s_buf[c] - jnp.where(valid, m_new, -NEG_INF)).astype(BF16))
            m_ref[c] = m_new

        def scores_stage(c):
            s_new = jnp.dot(k_ref[c, pl.ds(start, tk), :], q2_ref[c],
                            preferred_element_type=F32)
            if with_bias:
                s_new = s_new + b2
            s_buf[c] = s_new
            t_buf[c] = jnp.max(s_new, axis=0, keepdims=True)

        for c in chains:
            values_stage(c)
            if softmax:
                softmax_stage(c)
            if scores:
                scores_stage(c)

    n_far = jnp.maximum(last + 1 - n_band, 0)
    n_pairs = n_far // 2
    done = 0
    for pairs_per_trip in FAR_TRIP_PAIRS:
        def far_trip(u, carry, pairs_per_trip=pairs_per_trip):
            for d in range(2 * pairs_per_trip):
                time_step(2 * u * pairs_per_trip + d, d % 2)
            return carry

        trips = n_pairs // pairs_per_trip
        lax.fori_loop(done // pairs_per_trip, trips, far_trip, 0)
        done = trips * pairs_per_trip
    tau0 = 2 * n_pairs
    n_tail = n_band + 1
    before_drain()
    for d in range(n_tail + 4):
        time_step(tau0 + d, d % 2, scores=d < n_tail, softmax=d < n_tail + 2, with_bias=True)

    lam = _diff_lambda(lam_ref, lam_init)
    gain = g_ref[...] * (1.0 - lam_init)
    for c in chains:
        inv_l = 1.0 / acc_ref[c, V_DIM:V_DIM + 1, :]
        o1 = acc_ref[c, 0:V_DIM, 0:tq] * inv_l[:, 0:tq]
        o2 = acc_ref[c, 0:V_DIM, tq:2 * tq] * inv_l[:, tq:2 * tq]
        ot = o1 - lam * o2
        ms = jnp.mean(ot * ot, axis=0, keepdims=True)
        o_ref[c] = (ot * lax.rsqrt(ms + EPS) * gain).T.astype(BF16)


def _decode_stages(relb_ref, lam_ref, g_ref, q_ref, kn_ref, vn_ref, kc_refs, vc_refs, o_ref,
                   m_ref, l_ref, acc_ref, qm_ref, pageb_ref, newb_ref,
                   *, dec_seq, lam_init, pages_per_update):
    group = len(kc_refs)
    rows = N_HEADS * 2 * dec_seq
    per_head = 2 * dec_seq
    assert per_head == V7X_SUBLANES
    flat = PAGE_SIZE * N_HEADS
    nt = (((1,), (1,)), ((), ()))

    def grids(width):
        r = lax.broadcasted_iota(jnp.int32, (rows, width), 0)
        c = lax.broadcasted_iota(jnp.int32, (rows, width), 1)
        same_head = (c % N_HEADS) == (r // per_head)
        return r % dec_seq, c // N_HEADS, same_head

    def head_bias(dist):
        parts = [_shifted_bias(dist[h * per_head:(h + 1) * per_head], relb_ref, h)
                 for h in range(N_HEADS)]
        return jnp.concatenate(parts, axis=0)

    def begin(build_tables, first):
        @pl.when(build_tables)
        def _tables():
            tq, tk, ok = grids(flat)
            pageb_ref[0] = jnp.where(ok, 0.0, NEG_INF)
            pageb_ref[1] = jnp.where(ok, head_bias(PAGE_SIZE + tq - tk), NEG_INF)
            tqn, tkn, okn = grids(kn_ref.shape[0])
            newb_ref[...] = jnp.where(okn & (tkn <= tqn), head_bias(tqn - tkn), NEG_INF)

        @pl.when(first)
        def _init():
            m_ref[...] = jnp.full(m_ref.shape, NEG_INF, F32)
            l_ref[...] = jnp.zeros(l_ref.shape, F32)
            acc_ref[...] = jnp.zeros(acc_ref.shape, F32)
            q = q_ref[...]
            row = lax.broadcasted_iota(jnp.int32, (rows, QK_DIM), 0)
            lane = lax.broadcasted_iota(jnp.int32, (rows, QK_DIM), 1)
            first_comp = (row // dec_seq) % 2 == 0
            qm_ref[...] = jnp.where((lane < HEAD_DIM) == first_comp, q, jnp.zeros_like(q))

    def update(s_list, v_list):
        m_prev = m_ref[...]
        m_tile = functools.reduce(jnp.maximum, [jnp.max(s, axis=1, keepdims=True) for s in s_list])
        m_new = jnp.maximum(m_prev, m_tile)
        alpha = jnp.exp2(m_prev - m_new)
        p_list = [jnp.exp2(s - m_new[:, 0:1]) for s in s_list]
        l_ref[...] = alpha * l_ref[...] + sum(jnp.sum(p, axis=1, keepdims=True) for p in p_list)
        pv = sum(jnp.dot(p.astype(BF16), v(), preferred_element_type=F32)
                 for p, v in zip(p_list, v_list))
        acc_ref[...] = alpha * acc_ref[...] + pv
        m_ref[...] = m_new

    def scores(keys, bias):
        return lax.dot_general(qm_ref[...], keys, nt, preferred_element_type=F32) + bias

    def scores_t(keys_t, bias):
        return jnp.dot(qm_ref[...], keys_t, preferred_element_type=F32) + bias

    def pages(last):
        for lo in range(0, group, pages_per_update):
            hi = lo + pages_per_update
            slots = [0] * (hi - lo - 1) + [last.astype(jnp.int32) if hi == group else 0]
            update([scores_t(kc[...].reshape(flat, QK_DIM).T.astype(BF16), pageb_ref[slot])
                    for kc, slot in zip(kc_refs[lo:hi], slots)],
                   [lambda vc=vc: vc[...].reshape(flat, V_DIM).astype(BF16) for vc in vc_refs[lo:hi]])

    def end(last):
        @pl.when(last)
        def _finish():
            update([scores(kn_ref[...], newb_ref[...])], [lambda: vn_ref[...]])
            o = acc_ref[...] / l_ref[...]
            o2 = pltpu.roll(o, rows - dec_seq, axis=0)
            od = o - _diff_lambda(lam_ref, lam_init) * o2
            o_ref[...] = _head_norm(od, g_ref[...], lam_init)

    return begin, pages, end


N_PROMPT_SCRATCH = 12
N_DECODE_SCRATCH = 6


def _attn_kernel(pt_ref, relb_ref, lam_ref, gcol_ref, grow_ref, qt_ref, k_ref, vt_ref,
                 qs_ref, kn_ref, vn_ref, *rest, tq, tk, lam_init, dec_seq, pages, steps_per_row):
    del pt_ref
    kc_refs, vc_refs = rest[:pages], rest[pages:2 * pages]
    op_ref, os_ref = rest[2 * pages:2 * pages + 2]
    scratch = rest[2 * pages + 2:]
    prompt_scratch, decode_scratch = scratch[:N_PROMPT_SCRATCH], scratch[N_PROMPT_SCRATCH:]
    head, qi = pl.program_id(0), pl.program_id(1)
    step = head * pl.num_programs(1) + qi
    sub = step % steps_per_row
    last = sub == steps_per_row - 1
    decode_begin, decode_pages, decode_end = _decode_stages(
        relb_ref, lam_ref, grow_ref, qs_ref, kn_ref, vn_ref, kc_refs, vc_refs, os_ref, *decode_scratch,
        dec_seq=dec_seq, lam_init=lam_init, pages_per_update=SAMPLE_PAGES_PER_UPDATE)
    decode_begin(step == 0, sub == 0)
    _prompt_tile(relb_ref, lam_ref, gcol_ref, qt_ref, k_ref, vt_ref, op_ref, *prompt_scratch,
                 tq=tq, tk=tk, lam_init=lam_init, head=head, qi=qi,
                 before_drain=lambda: decode_pages(last))
    decode_end(last)


def _attn_call(qt, kb, vt, q_rows, k_new, v_new, cache_k, cache_v, page_table, rel_bias, lam_vecs,
               subln_g, lam_init, dec_seq, tq, tk):
    b, _, _, s = qt.shape
    nb, rows, _ = q_rows.shape
    n_pages = page_table.shape[1]
    n_steps = N_HEADS * (s // tq)
    assert s % tq == 0 and tq % tk == 0 and tk >= FAR_DISTANCE and PAGE_SIZE >= FAR_DISTANCE + dec_seq
    assert (nb * n_pages) % n_steps == 0
    pages = nb * n_pages // n_steps
    assert n_pages % pages == 0 and pages % SAMPLE_PAGES_PER_UPDATE == 0
    steps_per_row = n_pages // pages
    flat = PAGE_SIZE * N_HEADS
    n_q = s // tq

    def row_of(h, qi):
        step = h * n_q + qi
        return step // steps_per_row, (step % steps_per_row) * pages

    const = lambda shape: pl.BlockSpec(shape, lambda h, qi, pt: (0,) * len(shape),
                                       pipeline_mode=pl.Buffered(1))
    per_row = lambda shape: pl.BlockSpec((None,) + shape, lambda h, qi, pt: (row_of(h, qi)[0], 0, 0))
    paged = lambda g: pl.BlockSpec(
        (None, PAGE_SIZE, N_HEADS, QK_DIM),
        lambda h, qi, pt: (pt[row_of(h, qi)[0], row_of(h, qi)[1] + g], 0, 0, 0))
    out_tile = pl.BlockSpec((b, None, tq, V_DIM), lambda h, qi, pt: (0, h, qi, 0))
    q_tile = pl.BlockSpec((b, None, QK_DIM, tq), lambda h, qi, pt: (0, h, 0, qi))
    k_seq = pl.BlockSpec((b, None, s, QK_DIM), lambda h, qi, pt: (0, h, 0, 0), pipeline_mode=pl.Buffered(1))
    v_seq = pl.BlockSpec((b, None, V_EXT, s), lambda h, qi, pt: (0, h, 0, 0), pipeline_mode=pl.Buffered(1))
    stat = pltpu.VMEM((b, 1, 2 * tq), F32)
    prompt_scratch = [pltpu.VMEM((b, QK_DIM, 2 * tq), BF16), stat,
                      pltpu.VMEM((b, V_EXT, 2 * tq), F32),
                      pltpu.VMEM((tq // tk + 2, tk, tq), F32),
                      pltpu.VMEM((b, tk, 2 * tq), F32), pltpu.VMEM((b, tk, 2 * tq), F32),
                      pltpu.VMEM((b, tk, 2 * tq), BF16), pltpu.VMEM((b, tk, 2 * tq), BF16),
                      stat, stat, stat, stat]
    decode_scratch = [pltpu.VMEM((rows, V7X_LANES), F32),
                      pltpu.VMEM((rows, V7X_LANES), F32),
                      pltpu.VMEM((rows, V_DIM), F32),
                      pltpu.VMEM((rows, QK_DIM), BF16),
                      pltpu.VMEM((2, rows, flat), F32),
                      pltpu.VMEM((rows, k_new.shape[1]), F32)]
    assert len(prompt_scratch) == N_PROMPT_SCRATCH and len(decode_scratch) == N_DECODE_SCRATCH
    grid_spec = pltpu.PrefetchScalarGridSpec(
        num_scalar_prefetch=1,
        grid=(N_HEADS, n_q),
        in_specs=[pl.BlockSpec(memory_space=pltpu.SMEM), const((4, HEAD_DIM)), const((V_DIM, 1)),
                  const((1, V_DIM)), q_tile, k_seq, v_seq,
                  per_row((rows, QK_DIM)), per_row(k_new.shape[1:]), per_row(v_new.shape[1:])]
        + [paged(g) for g in range(pages)] * 2,
        out_specs=[out_tile, per_row((rows, V_DIM))],
        scratch_shapes=prompt_scratch + decode_scratch,
    )
    return pl.pallas_call(
        functools.partial(_attn_kernel, tq=tq, tk=tk, lam_init=lam_init, dec_seq=dec_seq, pages=pages,
                          steps_per_row=steps_per_row),
        grid_spec=grid_spec,
        out_shape=[jax.ShapeDtypeStruct((b, N_HEADS, s, V_DIM), BF16),
                   jax.ShapeDtypeStruct((nb, rows, V_DIM), F32)],
        compiler_params=pltpu.CompilerParams(dimension_semantics=("arbitrary",) * 2,
                                             vmem_limit_bytes=ATTN_VMEM_LIMIT_BYTES),
        name="attention",
    )(page_table, rel_bias, lam_vecs, subln_g.reshape(V_DIM, 1), subln_g.reshape(1, V_DIM), qt, kb, vt,
      q_rows, k_new, v_new, *([cache_k] * pages), *([cache_v] * pages))


def _ffn_kernel(*refs, d_ff, with_proj):
    if with_proj:
        x_ref, a_ref, wo_ref, gmix_ref, gpre_ref, wgu_ref, wd_ref, gpost_ref, o_ref = refs
        a = jnp.concatenate([a_ref[h] for h in range(N_HEADS)], axis=1)
        x = x_ref[...] + _rms(jnp.dot(a, wo_ref[...], preferred_element_type=F32), gmix_ref[...])
    else:
        x_ref, gpre_ref, wgu_ref, wd_ref, gpost_ref, o_ref = refs
        x = x_ref[...]
    hb = _rms(x, gpre_ref[...]).astype(BF16)
    g = jnp.dot(hb, wgu_ref[:, 0:d_ff], preferred_element_type=F32)
    u = jnp.dot(hb, wgu_ref[:, d_ff:2 * d_ff], preferred_element_type=F32)
    a = (g * (1.0 / (1.0 + jnp.exp(-g))) * u).astype(BF16)
    y = jnp.dot(a, wd_ref[...], preferred_element_type=F32)
    o_ref[...] = x + _rms(y, gpost_ref[...])


def _ffn_call(x, g_pre, wgu_bf16, wd_bf16, g_post, tm, proj=None):
    m = x.shape[0]
    d_ff = wd_bf16.shape[0]
    row = lambda width: pl.BlockSpec((tm, width), lambda i: (i, 0))
    ffn_specs = [_const_spec((1, D_MODEL)), _const_spec(wgu_bf16.shape),
                 _const_spec(wd_bf16.shape), _const_spec((1, D_MODEL))]
    ffn_args = (g_pre, wgu_bf16, wd_bf16, g_post)
    if proj is None:
        specs, args = [row(D_MODEL)] + ffn_specs, (x,) + ffn_args
    else:
        a_bf16, wo_bf16, g_mix = proj
        tiles_per_seq = a_bf16.shape[2] // tm
        heads = pl.BlockSpec((None, N_HEADS, tm, V_DIM),
                             lambda i: (i // tiles_per_seq, 0, i % tiles_per_seq, 0))
        specs = [row(D_MODEL), heads, _const_spec(wo_bf16.shape),
                 _const_spec((1, D_MODEL))] + ffn_specs
        args = (x, a_bf16, wo_bf16, g_mix) + ffn_args
    return pl.pallas_call(
        functools.partial(_ffn_kernel, d_ff=d_ff, with_proj=proj is not None),
        grid=(m // tm,),
        in_specs=specs,
        out_specs=row(D_MODEL),
        out_shape=jax.ShapeDtypeStruct((m, D_MODEL), F32),
        compiler_params=_params(1),
        name="swiglu_ffn",
    )(*args)


def _conv_kernel(*refs, tm, seq):
    carry_mode = seq >= tm
    if carry_mode:
        (x_ref, gpre_ref, win_ref, cw_ref, wout_ref, gpost_ref, st_ref,
         o_ref, tail_ref, carry_ref) = refs
    else:
        (x_ref, gpre_ref, win_ref, cw_ref, wout_ref, gpost_ref, inj1_ref, inj2_ref,
         o_ref, u_ref) = refs
    d = D_MODEL
    x = x_ref[...]
    hb = _rms(x, gpre_ref[...]).astype(BF16)
    gc = jnp.dot(hb, win_ref[:, d:2 * d], preferred_element_type=F32)
    xt = jnp.dot(hb, win_ref[:, 2 * d:3 * d], preferred_element_type=F32)
    u = gc * xt
    row = lax.broadcasted_iota(jnp.int32, (tm, d), 0)
    r1 = pltpu.roll(u, 1, axis=0)
    r2 = pltpu.roll(u, 2, axis=0)
    if carry_mode:
        i = pl.program_id(0)
        tiles_per_seq = seq // tm

        @pl.when(i % tiles_per_seq == 0)
        def _seq_start():
            carry_ref[...] = st_ref[...]

        c = carry_ref[...]
        c1 = c[V7X_SUBLANES - 1:V7X_SUBLANES]
        c2 = c[V7X_SUBLANES - 2:V7X_SUBLANES - 1]
        prev1 = jnp.where(row == 0, c1, r1)
        prev2 = jnp.where(row == 0, c2, jnp.where(row == 1, c1, r2))
        tail = u[tm - V7X_SUBLANES:tm]
        carry_ref[...] = tail
        tail_ref[...] = tail
    else:
        t = row % seq
        prev1 = jnp.where(t >= 1, r1, inj1_ref[...])
        prev2 = jnp.where(t >= 2, r2, inj2_ref[...])
        u_ref[...] = u
    cw = cw_ref[...]
    y = cw[0:1] * prev2 + cw[1:2] * prev1 + cw[2:3] * u
    gb = jnp.dot(hb, win_ref[:, 0:d], preferred_element_type=F32)
    z = jnp.dot((gb * y).astype(BF16), wout_ref[...], preferred_element_type=F32)
    o_ref[...] = x + _rms(z, gpost_ref[...])


def _conv_call(x, g_pre, win_bf16, conv_w, wout_bf16, g_post, state, tm, seq):
    m = x.shape[0]
    nb = m // seq
    row = pl.BlockSpec((tm, D_MODEL), lambda i: (i, 0))
    common = [row, _const_spec((1, D_MODEL)), _const_spec(win_bf16.shape),
              _const_spec((CONV_WIDTH, D_MODEL)), _const_spec(wout_bf16.shape),
              _const_spec((1, D_MODEL))]
    kern = functools.partial(_conv_kernel, tm=tm, seq=seq)
    if seq >= tm:
        assert seq % tm == 0
        n_tiles = m // tm
        tiles_per_seq = seq // tm
        pad = jnp.zeros((nb, V7X_SUBLANES - (CONV_WIDTH - 1), D_MODEL), F32)
        st = jnp.concatenate([pad, state], axis=1)
        st_spec = pl.BlockSpec((None, V7X_SUBLANES, D_MODEL), lambda i: (i // tiles_per_seq, 0, 0))
        tail_spec = pl.BlockSpec((None, V7X_SUBLANES, D_MODEL), lambda i: (i, 0, 0))
        out, tails = pl.pallas_call(
            kern, grid=(n_tiles,),
            in_specs=common + [st_spec],
            out_specs=[row, tail_spec],
            out_shape=[jax.ShapeDtypeStruct((m, D_MODEL), F32),
                       jax.ShapeDtypeStruct((n_tiles, V7X_SUBLANES, D_MODEL), F32)],
            scratch_shapes=[pltpu.VMEM((V7X_SUBLANES, D_MODEL), F32)],
            compiler_params=_params(1),
            name="short_conv_seq",
        )(x, g_pre, win_bf16, conv_w, wout_bf16, g_post, st)
        last = tails.reshape(nb, tiles_per_seq, V7X_SUBLANES, D_MODEL)[:, -1]
        return out, last[:, V7X_SUBLANES - (CONV_WIDTH - 1):]
    assert tm % seq == 0 and m == tm and seq >= CONV_WIDTH - 1
    zeros = lambda n: jnp.zeros((nb, n, D_MODEL), F32)
    inj1 = jnp.concatenate([state[:, 1:2], zeros(seq - 1)], axis=1).reshape(m, D_MODEL)
    inj2 = jnp.concatenate([state[:, 0:2], zeros(seq - 2)], axis=1).reshape(m, D_MODEL)
    out, u = pl.pallas_call(
        kern, grid=(1,),
        in_specs=common + [row, row],
        out_specs=[row, row],
        out_shape=[jax.ShapeDtypeStruct((m, D_MODEL), F32)] * 2,
        compiler_params=_params(1),
        name="short_conv_tok",
    )(x, g_pre, win_bf16, conv_w, wout_bf16, g_post, inj1, inj2)
    return out, u.reshape(nb, seq, D_MODEL)[:, seq - (CONV_WIDTH - 1):]


def kernel(x_prompt, x_sample, cache_k, cache_v, state_conv, page_table, rel_bias, w_qkv, w_o, lambda_q1, lambda_k1, lambda_q2, lambda_k2, subln_g, w_conv_in, conv_w, w_conv_out, norm_mix_pre, norm_mix_post, norm_ffn_pre, norm_ffn_post, w_ffn_gu, w_ffn_down):
    b, s, d = x_prompt.shape
    nb, dec_seq, _ = x_sample.shape
    assert d == D_MODEL and page_table.shape[1] * PAGE_SIZE >= FAR_DISTANCE
    mp, ms = b * s, nb * dec_seq
    tm_p = TOKEN_TILE

    wqkv = w_qkv.astype(BF16)
    wo = w_o.astype(BF16)
    win = w_conv_in.astype(BF16)
    wout = w_conv_out.astype(BF16)
    wgu = w_ffn_gu.astype(BF16)
    wd = w_ffn_down.astype(BF16)
    gain = lambda g, i: g[i].reshape(1, D_MODEL)
    lam_vecs = jnp.stack([lambda_q1, lambda_k1, lambda_q2, lambda_k2])
    lam0 = _lambda_init(0)

    xp = x_prompt.reshape(mp, d)
    xs = x_sample.reshape(ms, d)

    g_pre = gain(norm_mix_pre, 0)
    kf_p, vf_p, kb_p, qt_p, vt_p = _qkv_call(xp, g_pre, wqkv, tm_p, s)
    kf_s, vf_s, kb_s, qt_s, _ = _qkv_call(xs, g_pre, wqkv, ms, ms)

    q4 = qt_s.reshape(N_HEADS, QK_DIM, nb, dec_seq).transpose(2, 0, 3, 1)
    q_rows = jnp.stack([q4, q4], axis=2).reshape(nb, N_HEADS * 2 * dec_seq, QK_DIM)
    pad_tok = PAGE_SIZE // N_HEADS - dec_seq
    flat_new = lambda a: jnp.pad(a.reshape(nb, dec_seq, N_HEADS, QK_DIM),
                                 ((0, 0), (0, pad_tok), (0, 0), (0, 0))).reshape(nb, PAGE_SIZE, QK_DIM)
    on_p, on_rows = _attn_call(qt_p, kb_p, vt_p, q_rows, flat_new(kb_s[0].transpose(1, 0, 2)),
                               flat_new(vf_s.astype(BF16)), cache_k, cache_v, page_table, rel_bias,
                               lam_vecs, subln_g, lam0, dec_seq, ATTN_QUERY_TILE, ATTN_KEY_TILE)
    on_s = on_rows.reshape(nb, N_HEADS, 2, dec_seq, V_DIM)[:, :, 0].transpose(1, 0, 2, 3)
    on_s = on_s.reshape(1, N_HEADS, ms, V_DIM).astype(BF16)

    g_post = gain(norm_mix_post, 0)
    ffn = lambda x, i, tm, proj=None: _ffn_call(x, gain(norm_ffn_pre, i), wgu[i], wd[i],
                                                gain(norm_ffn_post, i), tm, proj)
    xp = ffn(xp, 0, tm_p, (on_p, wo, g_post))
    xs = ffn(xs, 0, ms, (on_s, wo, g_post))

    g_pre, g_post = gain(norm_mix_pre, 1), gain(norm_mix_post, 1)
    zero_state = jnp.zeros((b, CONV_WIDTH - 1, d), F32)
    xp, conv_p = _conv_call(xp, g_pre, win, conv_w, wout, g_post, zero_state, tm_p, s)
    xs, conv_s = _conv_call(xs, g_pre, win, conv_w, wout, g_post, state_conv, ms, dec_seq)
    xp = ffn(xp, 1, tm_p)
    xs = ffn(xs, 1, ms)

    return (xp.reshape(b, s, d), xs.reshape(nb, dec_seq, d),
            kf_p.reshape(b, s, N_HEADS, QK_DIM), vf_p.reshape(b, s, N_HEADS, V_DIM),
            kf_s.reshape(nb, dec_seq, N_HEADS, QK_DIM), vf_s.reshape(nb, dec_seq, N_HEADS, V_DIM),
            conv_p, conv_s)
```

```python
import functools
import math

import jax
import jax.numpy as jnp
from jax import lax
from jax.experimental import pallas as pl
from jax.experimental.pallas import tpu as pltpu

D_MODEL = 1024
N_HEADS = 8
HEAD_DIM = 64
QK_DIM = 2 * HEAD_DIM
V_DIM = 2 * HEAD_DIM
ATTN_WIDTH = N_HEADS * V_DIM
N_BUCKETS = 32
MAX_DISTANCE = 128
CONV_WIDTH = 3
PAGE_SIZE = 128
EPS = 1e-6
NEG_INF = -1e30
SCALE = HEAD_DIM ** -0.5
LOG2E = math.log2(math.e)

V7X_LANES = 128
V7X_BF16_SUBLANES = 16
V_EXT = V_DIM + V7X_BF16_SUBLANES
V7X_SUBLANES = 8
VMEM_LIMIT_BYTES = 56 * 1024 * 1024
SAMPLE_PAGES_PER_UPDATE = 8
ATTN_VMEM_LIMIT_BYTES = 60 * 1024 * 1024
FFN_STREAM_CHUNK = 256
TOKEN_TILE = 512
ATTN_QUERY_TILE = 512
ATTN_KEY_TILE = 256
FAR_TRIP_PAIRS = (2, 1)
F32 = jnp.float32
BF16 = jnp.bfloat16


def _lambda_init(layer):
    return 0.8 - 0.6 * math.exp(-0.3 * layer)


def _bucket_uppers():
    max_exact = N_BUCKETS // 2
    n_far = None
    buckets = []
    for n in range(0, 4 * MAX_DISTANCE):
        if n < max_exact:
            b = n
        else:
            r = math.log(n / max_exact) / math.log(MAX_DISTANCE / max_exact) * (N_BUCKETS - max_exact)
            frac = r - math.floor(r)
            assert min(frac, 1 - frac) > 1e-4 or n == max_exact or n >= MAX_DISTANCE, n
            b = min(max_exact + int(r + (1e-6 if n == max_exact else 0.0)), N_BUCKETS - 1)
            if n >= MAX_DISTANCE:
                b = N_BUCKETS - 1
        buckets.append(b)
    uppers = []
    for b in range(N_BUCKETS - 1):
        uppers.append(next(n for n, bb in enumerate(buckets) if bb > b))
    n_far = uppers[-1]
    return tuple(uppers), n_far


BUCKET_UPPERS, FAR_DISTANCE = _bucket_uppers()


def _shifted_bias(n, relb_ref, head):
    far = relb_ref[N_BUCKETS - 1, head]
    val = jnp.zeros(n.shape, F32)
    for b in range(N_BUCKETS - 2, -1, -1):
        val = jnp.where(n < BUCKET_UPPERS[b], (relb_ref[b, head] - far) * LOG2E, val)
    return val


def _diff_lambda(lam_ref, lam_init):
    lv = lam_ref[...]
    a = jnp.sum(lv[0:1] * lv[1:2], axis=1, keepdims=True)
    b = jnp.sum(lv[2:3] * lv[3:4], axis=1, keepdims=True)
    return jnp.exp(a) - jnp.exp(b) + lam_init


def _rms(x, g):
    return x * lax.rsqrt(jnp.mean(x * x, axis=-1, keepdims=True) + EPS) * g


def _head_norm(o, g, lam_init):
    return o * lax.rsqrt(jnp.mean(o * o, axis=-1, keepdims=True) + EPS) * g * (1.0 - lam_init)


def _const_spec(shape):
    return pl.BlockSpec(shape, lambda *_: (0,) * len(shape), pipeline_mode=pl.Buffered(1))


def _params(n_grid):
    return pltpu.CompilerParams(dimension_semantics=("arbitrary",) * n_grid,
                                vmem_limit_bytes=VMEM_LIMIT_BYTES)


def _qkv_kernel(x_ref, g_ref, w_ref, kf_ref, vf_ref, kb_ref, qt_ref, vt_ref):
    hb = _rms(x_ref[...], g_ref[...]).astype(BF16)
    w = ATTN_WIDTH
    tm = hb.shape[0]
    q = jnp.dot(hb, w_ref[:, 0:w], preferred_element_type=F32)
    qt_ref[...] = (q * (SCALE * LOG2E)).T.astype(BF16).reshape(N_HEADS, QK_DIM, tm)
    k = jnp.dot(hb, w_ref[:, w:2 * w], preferred_element_type=F32)
    kf_ref[...] = k
    for h in range(N_HEADS):
        kb_ref[h] = k[:, h * QK_DIM:(h + 1) * QK_DIM].astype(BF16)
    v = jnp.dot(hb, w_ref[:, 2 * w:3 * w], preferred_element_type=F32)
    vf_ref[...] = v
    vt_ref[:, 0:V_DIM, :] = v.T.astype(BF16).reshape(N_HEADS, V_DIM, tm)
    vt_ref[:, V_DIM:V_EXT, :] = jnp.ones((N_HEADS, V_EXT - V_DIM, tm), BF16)


def _qkv_call(x, g, w_bf16, tm, seq):
    m = x.shape[0]
    tiles_per_seq = seq // tm
    row = lambda width: pl.BlockSpec((tm, width), lambda i: (i, 0))
    fmaj = lambda rows: pl.BlockSpec((None, N_HEADS, rows, tm),
                                     lambda i: (i // tiles_per_seq, 0, 0, i % tiles_per_seq))
    hmaj = pl.BlockSpec((None, N_HEADS, tm, QK_DIM),
                        lambda i: (i // tiles_per_seq, 0, i % tiles_per_seq, 0))
    return pl.pallas_call(
        _qkv_kernel,
        grid=(m // tm,),
        in_specs=[row(D_MODEL), _const_spec((1, D_MODEL)), _const_spec(w_bf16.shape)],
        out_specs=[row(ATTN_WIDTH)] * 2 + [hmaj, fmaj(QK_DIM), fmaj(V_EXT)],
        out_shape=[jax.ShapeDtypeStruct((m, ATTN_WIDTH), F32)] * 2
        + [jax.ShapeDtypeStruct((m // seq, N_HEADS, seq, QK_DIM), BF16),
           jax.ShapeDtypeStruct((m // seq, N_HEADS, QK_DIM, seq), BF16),
           jax.ShapeDtypeStruct((m // seq, N_HEADS, V_EXT, seq), BF16)],
        compiler_params=_params(1),
        name="qkv_proj",
    )(x, g, w_bf16)


def _prompt_tile(relb_ref, lam_ref, g_ref, qt_ref, k_ref, vt_ref, o_ref,
                 q2_ref, m_ref, acc_ref, bias_ref, s0_ref, s1_ref, p0_ref, p1_ref,
                 a0_ref, a1_ref, t0_ref, t1_ref, *, tq, tk, lam_init, head, qi, before_drain):
    chains = range(qt_ref.shape[0])
    ratio = tq // tk
    n_band = ratio + 1
    last = ratio * (qi + 1) - 1

    @pl.when(qi == 0)
    def _build_bias():
        j = lax.broadcasted_iota(jnp.int32, (tk, tq), 0)
        i = lax.broadcasted_iota(jnp.int32, (tk, tq), 1)
        for d in range(n_band):
            n = i - j + tk * (d + 1 - ratio)
            bias_ref[d] = jnp.where(n >= 0, _shifted_bias(n, relb_ref, head), NEG_INF)
        bias_ref[n_band] = jnp.zeros((tk, tq), F32)

    feat = lax.broadcasted_iota(jnp.int32, (QK_DIM, tq), 0)
    for c in chains:
        qt = qt_ref[c]
        zero = jnp.zeros_like(qt)
        q2_ref[c, :, 0:tq] = jnp.where(feat < HEAD_DIM, qt, zero)
        q2_ref[c, :, tq:2 * tq] = jnp.where(feat >= HEAD_DIM, qt, zero)
    m_ref[...] = jnp.full(m_ref.shape, NEG_INF, F32)
    acc_ref[...] = jnp.zeros(acc_ref.shape, F32)
    for p_buf, a_buf in ((p0_ref, a0_ref), (p1_ref, a1_ref)):
        p_buf[...] = jnp.zeros(p_buf.shape, BF16)
        a_buf[...] = jnp.ones(a_buf.shape, F32)

    @pl.when((head == 0) & (qi == 0))
    def _first_touch():
        for s_buf, t_buf in ((s0_ref, t0_ref), (s1_ref, t1_ref)):
            s_buf[...] = jnp.zeros(s_buf.shape, F32)
            t_buf[...] = jnp.zeros(t_buf.shape, F32)

    def time_step(tau, parity, *, scores=True, softmax=True, with_bias=False):
        s_buf, t_buf = (s0_ref, t0_ref) if parity == 0 else (s1_ref, t1_ref)
        p_buf, a_buf = (p0_ref, a0_ref) if parity == 0 else (p1_ref, a1_ref)
        jv = pl.multiple_of(jnp.clip(tau - 4, 0, last) * tk, tk)
        valid = (tau >= 2) & (tau - 2 <= last)
        jq = jnp.minimum(tau, last)
        start = pl.multiple_of(jq * tk, tk)
        if scores and with_bias:
            b = bias_ref[jnp.minimum(last - jq, n_band)]
            b2 = jnp.concatenate([b, b], axis=1)

        def values_stage(c):
            pv = jnp.dot(vt_ref[c, :, pl.ds(jv, tk)], p_buf[c], preferred_element_type=F32)
            acc_ref[c] = a_buf[c] * acc_ref[c] + pv

        def softmax_stage(c):
            m_prev = m_ref[c]
            m_new = jnp.where(valid, jnp.maximum(m_prev, t_buf[c]), m_prev)
            a_buf[c] = jnp.exp2(m_prev - m_new)
            p_buf[c] = jnp.exp2((s_buf[c] - jnp.where(valid, m_new, -NEG_INF)).astype(BF16))
            m_ref[c] = m_new

        def scores_stage(c):
            s_new = jnp.dot(k_ref[c, pl.ds(start, tk), :], q2_ref[c],
                            preferred_element_type=F32)
            if with_bias:
                s_new = s_new + b2
            s_buf[c] = s_new
            t_buf[c] = jnp.max(s_new, axis=0, keepdims=True)

        for c in chains:
            values_stage(c)
            if softmax:
                softmax_stage(c)
            if scores:
                scores_stage(c)

    n_far = jnp.maximum(last + 1 - n_band, 0)
    n_pairs = n_far // 2
    done = 0
    for pairs_per_trip in FAR_TRIP_PAIRS:
        def far_trip(u, carry, pairs_per_trip=pairs_per_trip):
            for d in range(2 * pairs_per_trip):
                time_step(2 * u * pairs_per_trip + d, d % 2)
            return carry

        trips = n_pairs // pairs_per_trip
        lax.fori_loop(done // pairs_per_trip, trips, far_trip, 0)
        done = trips * pairs_per_trip
    tau0 = 2 * n_pairs
    n_tail = n_band + 1
    before_drain()
    for d in range(n_tail + 4):
        time_step(tau0 + d, d % 2, scores=d < n_tail, softmax=d < n_tail + 2, with_bias=True)

    lam = _diff_lambda(lam_ref, lam_init)
    gain = g_ref[...] * (1.0 - lam_init)
    for c in chains:
        inv_l = 1.0 / acc_ref[c, V_DIM:V_DIM + 1, :]
        o1 = acc_ref[c, 0:V_DIM, 0:tq] * inv_l[:, 0:tq]
        o2 = acc_ref[c, 0:V_DIM, tq:2 * tq] * inv_l[:, tq:2 * tq]
        ot = o1 - lam * o2
        ms = jnp.mean(ot * ot, axis=0, keepdims=True)
        o_ref[c] = (ot * lax.rsqrt(ms + EPS) * gain).T.astype(BF16)


def _decode_stages(relb_ref, lam_ref, g_ref, q_ref, kn_ref, vn_ref, kc_refs, vc_refs, o_ref,
                   m_ref, l_ref, acc_ref, qm_ref, pageb_ref, newb_ref,
                   *, dec_seq, lam_init, pages_per_update):
    group = len(kc_refs)
    rows = N_HEADS * 2 * dec_seq
    per_head = 2 * dec_seq
    assert per_head == V7X_SUBLANES
    flat = PAGE_SIZE * N_HEADS
    nt = (((1,), (1,)), ((), ()))

    def grids(width):
        r = lax.broadcasted_iota(jnp.int32, (rows, width), 0)
        c = lax.broadcasted_iota(jnp.int32, (rows, width), 1)
        same_head = (c % N_HEADS) == (r // per_head)
        return r % dec_seq, c // N_HEADS, same_head

    def head_bias(dist):
        parts = [_shifted_bias(dist[h * per_head:(h + 1) * per_head], relb_ref, h)
                 for h in range(N_HEADS)]
        return jnp.concatenate(parts, axis=0)

    def begin(build_tables, first):
        @pl.when(build_tables)
        def _tables():
            tq, tk, ok = grids(flat)
            pageb_ref[0] = jnp.where(ok, 0.0, NEG_INF)
            pageb_ref[1] = jnp.where(ok, head_bias(PAGE_SIZE + tq - tk), NEG_INF)
            tqn, tkn, okn = grids(kn_ref.shape[0])
            newb_ref[...] = jnp.where(okn & (tkn <= tqn), head_bias(tqn - tkn), NEG_INF)

        @pl.when(first)
        def _init():
            m_ref[...] = jnp.full(m_ref.shape, NEG_INF, F32)
            l_ref[...] = jnp.zeros(l_ref.shape, F32)
            acc_ref[...] = jnp.zeros(acc_ref.shape, F32)
            q = q_ref[...]
            row = lax.broadcasted_iota(jnp.int32, (rows, QK_DIM), 0)
            lane = lax.broadcasted_iota(jnp.int32, (rows, QK_DIM), 1)
            first_comp = (row // dec_seq) % 2 == 0
            qm_ref[...] = jnp.where((lane < HEAD_DIM) == first_comp, q, jnp.zeros_like(q))

    def update(s_list, v_list):
        m_prev = m_ref[...]
        m_tile = functools.reduce(jnp.maximum, [jnp.max(s, axis=1, keepdims=True) for s in s_list])
        m_new = jnp.maximum(m_prev, m_tile)
        alpha = jnp.exp2(m_prev - m_new)
        p_list = [jnp.exp2(s - m_new[:, 0:1]) for s in s_list]
        l_ref[...] = alpha * l_ref[...] + sum(jnp.sum(p, axis=1, keepdims=True) for p in p_list)
        pv = sum(jnp.dot(p.astype(BF16), v(), preferred_element_type=F32)
                 for p, v in zip(p_list, v_list))
        acc_ref[...] = alpha * acc_ref[...] + pv
        m_ref[...] = m_new

    def scores(keys, bias):
        return lax.dot_general(qm_ref[...], keys, nt, preferred_element_type=F32) + bias

    def scores_t(keys_t, bias):
        return jnp.dot(qm_ref[...], keys_t, preferred_element_type=F32) + bias

    def pages(last):
        for lo in range(0, group, pages_per_update):
            hi = lo + pages_per_update
            slots = [0] * (hi - lo - 1) + [last.astype(jnp.int32) if hi == group else 0]
            update([scores_t(kc[...].reshape(flat, QK_DIM).T.astype(BF16), pageb_ref[slot])
                    for kc, slot in zip(kc_refs[lo:hi], slots)],
                   [lambda vc=vc: vc[...].reshape(flat, V_DIM).astype(BF16) for vc in vc_refs[lo:hi]])

    def end(last):
        @pl.when(last)
        def _finish():
            update([scores(kn_ref[...], newb_ref[...])], [lambda: vn_ref[...]])
            o = acc_ref[...] / l_ref[...]
            o2 = pltpu.roll(o, rows - dec_seq, axis=0)
            od = o - _diff_lambda(lam_ref, lam_init) * o2
            o_ref[...] = _head_norm(od, g_ref[...], lam_init)

    return begin, pages, end


N_PROMPT_SCRATCH = 12
N_DECODE_SCRATCH = 6


def _attn_kernel(pt_ref, relb_ref, lam_ref, gcol_ref, grow_ref, qt_ref, k_ref, vt_ref,
                 qs_ref, kn_ref, vn_ref, *rest, tq, tk, lam_init, dec_seq, pages, steps_per_row):
    del pt_ref
    kc_refs, vc_refs = rest[:pages], rest[pages:2 * pages]
    op_ref, os_ref = rest[2 * pages:2 * pages + 2]
    scratch = rest[2 * pages + 2:]
    prompt_scratch, decode_scratch = scratch[:N_PROMPT_SCRATCH], scratch[N_PROMPT_SCRATCH:]
    head, qi = pl.program_id(0), pl.program_id(1)
    step = head * pl.num_programs(1) + qi
    sub = step % steps_per_row
    last = sub == steps_per_row - 1
    decode_begin, decode_pages, decode_end = _decode_stages(
        relb_ref, lam_ref, grow_ref, qs_ref, kn_ref, vn_ref, kc_refs, vc_refs, os_ref, *decode_scratch,
        dec_seq=dec_seq, lam_init=lam_init, pages_per_update=SAMPLE_PAGES_PER_UPDATE)
    decode_begin(step == 0, sub == 0)
    _prompt_tile(relb_ref, lam_ref, gcol_ref, qt_ref, k_ref, vt_ref, op_ref, *prompt_scratch,
                 tq=tq, tk=tk, lam_init=lam_init, head=head, qi=qi,
                 before_drain=lambda: decode_pages(last))
    decode_end(last)


def _attn_call(qt, kb, vt, q_rows, k_new, v_new, cache_k, cache_v, page_table, rel_bias, lam_vecs,
               subln_g, lam_init, dec_seq, tq, tk):
    b, _, _, s = qt.shape
    nb, rows, _ = q_rows.shape
    n_pages = page_table.shape[1]
    n_steps = N_HEADS * (s // tq)
    assert s % tq == 0 and tq % tk == 0 and tk >= FAR_DISTANCE and PAGE_SIZE >= FAR_DISTANCE + dec_seq
    assert (nb * n_pages) % n_steps == 0
    pages = nb * n_pages // n_steps
    assert n_pages % pages == 0 and pages % SAMPLE_PAGES_PER_UPDATE == 0
    steps_per_row = n_pages // pages
    flat = PAGE_SIZE * N_HEADS
    n_q = s // tq

    def row_of(h, qi):
        step = h * n_q + qi
        return step // steps_per_row, (step % steps_per_row) * pages

    const = lambda shape: pl.BlockSpec(shape, lambda h, qi, pt: (0,) * len(shape),
                                       pipeline_mode=pl.Buffered(1))
    per_row = lambda shape: pl.BlockSpec((None,) + shape, lambda h, qi, pt: (row_of(h, qi)[0], 0, 0))
    paged = lambda g: pl.BlockSpec(
        (None, PAGE_SIZE, N_HEADS, QK_DIM),
        lambda h, qi, pt: (pt[row_of(h, qi)[0], row_of(h, qi)[1] + g], 0, 0, 0))
    out_tile = pl.BlockSpec((b, None, tq, V_DIM), lambda h, qi, pt: (0, h, qi, 0))
    q_tile = pl.BlockSpec((b, None, QK_DIM, tq), lambda h, qi, pt: (0, h, 0, qi))
    k_seq = pl.BlockSpec((b, None, s, QK_DIM), lambda h, qi, pt: (0, h, 0, 0), pipeline_mode=pl.Buffered(1))
    v_seq = pl.BlockSpec((b, None, V_EXT, s), lambda h, qi, pt: (0, h, 0, 0), pipeline_mode=pl.Buffered(1))
    stat = pltpu.VMEM((b, 1, 2 * tq), F32)
    prompt_scratch = [pltpu.VMEM((b, QK_DIM, 2 * tq), BF16), stat,
                      pltpu.VMEM((b, V_EXT, 2 * tq), F32),
                      pltpu.VMEM((tq // tk + 2, tk, tq), F32),
                      pltpu.VMEM((b, tk, 2 * tq), F32), pltpu.VMEM((b, tk, 2 * tq), F32),
                      pltpu.VMEM((b, tk, 2 * tq), BF16), pltpu.VMEM((b, tk, 2 * tq), BF16),
                      stat, stat, stat, stat]
    decode_scratch = [pltpu.VMEM((rows, V7X_LANES), F32),
                      pltpu.VMEM((rows, V7X_LANES), F32),
                      pltpu.VMEM((rows, V_DIM), F32),
                      pltpu.VMEM((rows, QK_DIM), BF16),
                      pltpu.VMEM((2, rows, flat), F32),
                      pltpu.VMEM((rows, k_new.shape[1]), F32)]
    assert len(prompt_scratch) == N_PROMPT_SCRATCH and len(decode_scratch) == N_DECODE_SCRATCH
    grid_spec = pltpu.PrefetchScalarGridSpec(
        num_scalar_prefetch=1,
        grid=(N_HEADS, n_q),
        in_specs=[pl.BlockSpec(memory_space=pltpu.SMEM), const((4, HEAD_DIM)), const((V_DIM, 1)),
                  const((1, V_DIM)), q_tile, k_seq, v_seq,
                  per_row((rows, QK_DIM)), per_row(k_new.shape[1:]), per_row(v_new.shape[1:])]
        + [paged(g) for g in range(pages)] * 2,
        out_specs=[out_tile, per_row((rows, V_DIM))],
        scratch_shapes=prompt_scratch + decode_scratch,
    )
    return pl.pallas_call(
        functools.partial(_attn_kernel, tq=tq, tk=tk, lam_init=lam_init, dec_seq=dec_seq, pages=pages,
                          steps_per_row=steps_per_row),
        grid_spec=grid_spec,
        out_shape=[jax.ShapeDtypeStruct((b, N_HEADS, s, V_DIM), BF16),
                   jax.ShapeDtypeStruct((nb, rows, V_DIM), F32)],
        compiler_params=pltpu.CompilerParams(dimension_semantics=("arbitrary",) * 2,
                                             vmem_limit_bytes=ATTN_VMEM_LIMIT_BYTES),
        name="attention",
    )(page_table, rel_bias, lam_vecs, subln_g.reshape(V_DIM, 1), subln_g.reshape(1, V_DIM), qt, kb, vt,
      q_rows, k_new, v_new, *([cache_k] * pages), *([cache_v] * pages))


def _silu_mul(g, u):
    return (g * (1.0 / (1.0 + jnp.exp(-g))) * u).astype(BF16)


def _with_mixer_branch(x_ref, proj_refs):
    if not proj_refs:
        return x_ref[...]
    a_ref, wo_ref, gmix_ref = proj_refs
    a = jnp.concatenate([a_ref[h] for h in range(N_HEADS)], axis=1)
    return x_ref[...] + _rms(jnp.dot(a, wo_ref[...], preferred_element_type=F32), gmix_ref[...])


def _ffn_kernel(*refs, with_proj):
    x_ref, proj_refs = refs[0], (refs[1:4] if with_proj else ())
    gpre_ref, wg_ref, wu_ref, wd_ref, gpost_ref, o_ref = refs[1 + len(proj_refs):]
    x = _with_mixer_branch(x_ref, proj_refs)
    hb = _rms(x, gpre_ref[...]).astype(BF16)
    g = jnp.dot(hb, wg_ref[...], preferred_element_type=F32)
    u = jnp.dot(hb, wu_ref[...], preferred_element_type=F32)
    y = jnp.dot(_silu_mul(g, u), wd_ref[...], preferred_element_type=F32)
    o_ref[...] = x + _rms(y, gpost_ref[...])


def _ffn_stream_kernel(*refs, with_proj):
    x_ref, proj_refs = refs[0], (refs[1:4] if with_proj else ())
    (gpre_ref, wg_ref, wu_ref, wd_ref, gpost_ref,
     o_ref, wgb_ref, wub_ref, wdb_ref, x_acc_ref, hb_ref, y_ref) = refs[1 + len(proj_refs):]
    c = pl.program_id(0)

    @pl.when(c == 0)
    def _first_chunk():
        x = _with_mixer_branch(x_ref, proj_refs)
        x_acc_ref[...] = x
        hb_ref[...] = _rms(x, gpre_ref[...]).astype(BF16)
        y_ref[...] = jnp.zeros(y_ref.shape, F32)

    wg, wu, wd = wg_ref[...].astype(BF16), wu_ref[...].astype(BF16), wd_ref[...].astype(BF16)
    wgb_ref[...], wub_ref[...], wdb_ref[...] = wg, wu, wd
    hb = hb_ref[...]
    g = jnp.dot(hb, wg, preferred_element_type=F32)
    u = jnp.dot(hb, wu, preferred_element_type=F32)
    y_ref[...] += jnp.dot(_silu_mul(g, u), wd, preferred_element_type=F32)

    @pl.when(c == pl.num_programs(0) - 1)
    def _last_chunk():
        o_ref[...] = x_acc_ref[...] + _rms(y_ref[...], gpost_ref[...])


def _proj_specs(proj, tm, tile_of):
    a_bf16, wo_bf16, g_mix = proj
    tiles_per_seq = a_bf16.shape[2] // tm
    heads = pl.BlockSpec((None, N_HEADS, tm, V_DIM),
                         lambda i: (tile_of(i) // tiles_per_seq, 0, tile_of(i) % tiles_per_seq, 0))
    return [heads, _const_spec(wo_bf16.shape), _const_spec((1, D_MODEL))], (a_bf16, wo_bf16, g_mix)


def _ffn_call(x, g_pre, wg_bf16, wu_bf16, wd_bf16, g_post, tm, proj=None):
    m = x.shape[0]
    row = pl.BlockSpec((tm, D_MODEL), lambda i: (i, 0))
    proj_specs, proj_args = _proj_specs(proj, tm, lambda i: i) if proj else ([], ())
    specs = [row] + proj_specs + [_const_spec((1, D_MODEL)), _const_spec(wg_bf16.shape),
                                  _const_spec(wu_bf16.shape), _const_spec(wd_bf16.shape),
                                  _const_spec((1, D_MODEL))]
    return pl.pallas_call(
        functools.partial(_ffn_kernel, with_proj=proj is not None),
        grid=(m // tm,),
        in_specs=specs,
        out_specs=row,
        out_shape=jax.ShapeDtypeStruct((m, D_MODEL), F32),
        compiler_params=_params(1),
        name="swiglu_ffn",
    )(x, *proj_args, g_pre, wg_bf16, wu_bf16, wd_bf16, g_post)


def _ffn_stream_call(x, g_pre, w_gu, w_down, layer, g_post, proj=None):
    m = x.shape[0]
    d_ff = w_down.shape[1]
    tc = FFN_STREAM_CHUNK
    n_chunks = d_ff // tc
    assert d_ff % tc == 0
    whole = pl.BlockSpec((m, D_MODEL), lambda c: (0, 0))
    proj_specs, proj_args = _proj_specs(proj, m, lambda c: 0) if proj else ([], ())
    specs = [whole] + proj_specs + [
        _const_spec((1, D_MODEL)),
        pl.BlockSpec((None, D_MODEL, tc), lambda c: (layer, 0, c)),
        pl.BlockSpec((None, D_MODEL, tc), lambda c: (layer, 0, n_chunks + c)),
        pl.BlockSpec((None, tc, D_MODEL), lambda c: (layer, c, 0)),
        _const_spec((1, D_MODEL))]
    cols = pl.BlockSpec((D_MODEL, tc), lambda c: (0, c))
    return pl.pallas_call(
        functools.partial(_ffn_stream_kernel, with_proj=proj is not None),
        grid=(n_chunks,),
        in_specs=specs,
        out_specs=[whole, cols, cols, pl.BlockSpec((tc, D_MODEL), lambda c: (c, 0))],
        out_shape=[jax.ShapeDtypeStruct((m, D_MODEL), F32),
                   jax.ShapeDtypeStruct((D_MODEL, d_ff), BF16),
                   jax.ShapeDtypeStruct((D_MODEL, d_ff), BF16),
                   jax.ShapeDtypeStruct((d_ff, D_MODEL), BF16)],
        scratch_shapes=[pltpu.VMEM((m, D_MODEL), F32), pltpu.VMEM((m, D_MODEL), BF16),
                        pltpu.VMEM((m, D_MODEL), F32)],
        compiler_params=_params(1),
        name="swiglu_ffn_stream",
    )(x, *proj_args, g_pre, w_gu, w_gu, w_down, g_post)


def _conv_kernel(*refs, tm, seq):
    carry_mode = seq >= tm
    if carry_mode:
        (x_ref, gpre_ref, win_ref, cw_ref, wout_ref, gpost_ref, st_ref,
         o_ref, tail_ref, carry_ref) = refs
    else:
        (x_ref, gpre_ref, win_ref, cw_ref, wout_ref, gpost_ref, inj1_ref, inj2_ref,
         o_ref, u_ref) = refs
    d = D_MODEL
    x = x_ref[...]
    hb = _rms(x, gpre_ref[...]).astype(BF16)
    gc = jnp.dot(hb, win_ref[:, d:2 * d], preferred_element_type=F32)
    xt = jnp.dot(hb, win_ref[:, 2 * d:3 * d], preferred_element_type=F32)
    u = gc * xt
    row = lax.broadcasted_iota(jnp.int32, (tm, d), 0)
    r1 = pltpu.roll(u, 1, axis=0)
    r2 = pltpu.roll(u, 2, axis=0)
    if carry_mode:
        i = pl.program_id(0)
        tiles_per_seq = seq // tm

        @pl.when(i % tiles_per_seq == 0)
        def _seq_start():
            carry_ref[...] = st_ref[...]

        c = carry_ref[...]
        c1 = c[V7X_SUBLANES - 1:V7X_SUBLANES]
        c2 = c[V7X_SUBLANES - 2:V7X_SUBLANES - 1]
        prev1 = jnp.where(row == 0, c1, r1)
        prev2 = jnp.where(row == 0, c2, jnp.where(row == 1, c1, r2))
        tail = u[tm - V7X_SUBLANES:tm]
        carry_ref[...] = tail
        tail_ref[...] = tail
    else:
        t = row % seq
        prev1 = jnp.where(t >= 1, r1, inj1_ref[...])
        prev2 = jnp.where(t >= 2, r2, inj2_ref[...])
        u_ref[...] = u
    cw = cw_ref[...]
    y = cw[0:1] * prev2 + cw[1:2] * prev1 + cw[2:3] * u
    gb = jnp.dot(hb, win_ref[:, 0:d], preferred_element_type=F32)
    z = jnp.dot((gb * y).astype(BF16), wout_ref[...], preferred_element_type=F32)
    o_ref[...] = x + _rms(z, gpost_ref[...])


def _conv_call(x, g_pre, win_bf16, conv_w, wout_bf16, g_post, state, tm, seq):
    m = x.shape[0]
    nb = m // seq
    row = pl.BlockSpec((tm, D_MODEL), lambda i: (i, 0))
    common = [row, _const_spec((1, D_MODEL)), _const_spec(win_bf16.shape),
              _const_spec((CONV_WIDTH, D_MODEL)), _const_spec(wout_bf16.shape),
              _const_spec((1, D_MODEL))]
    kern = functools.partial(_conv_kernel, tm=tm, seq=seq)
    if seq >= tm:
        assert seq % tm == 0
        n_tiles = m // tm
        tiles_per_seq = seq // tm
        pad = jnp.zeros((nb, V7X_SUBLANES - (CONV_WIDTH - 1), D_MODEL), F32)
        st = jnp.concatenate([pad, state], axis=1)
        st_spec = pl.BlockSpec((None, V7X_SUBLANES, D_MODEL), lambda i: (i // tiles_per_seq, 0, 0))
        tail_spec = pl.BlockSpec((None, V7X_SUBLANES, D_MODEL), lambda i: (i, 0, 0))
        out, tails = pl.pallas_call(
            kern, grid=(n_tiles,),
            in_specs=common + [st_spec],
            out_specs=[row, tail_spec],
            out_shape=[jax.ShapeDtypeStruct((m, D_MODEL), F32),
                       jax.ShapeDtypeStruct((n_tiles, V7X_SUBLANES, D_MODEL), F32)],
            scratch_shapes=[pltpu.VMEM((V7X_SUBLANES, D_MODEL), F32)],
            compiler_params=_params(1),
            name="short_conv_seq",
        )(x, g_pre, win_bf16, conv_w, wout_bf16, g_post, st)
        last = tails.reshape(nb, tiles_per_seq, V7X_SUBLANES, D_MODEL)[:, -1]
        return out, last[:, V7X_SUBLANES - (CONV_WIDTH - 1):]
    assert tm % seq == 0 and m == tm and seq >= CONV_WIDTH - 1
    zeros = lambda n: jnp.zeros((nb, n, D_MODEL), F32)
    inj1 = jnp.concatenate([state[:, 1:2], zeros(seq - 1)], axis=1).reshape(m, D_MODEL)
    inj2 = jnp.concatenate([state[:, 0:2], zeros(seq - 2)], axis=1).reshape(m, D_MODEL)
    out, u = pl.pallas_call(
        kern, grid=(1,),
        in_specs=common + [row, row],
        out_specs=[row, row],
        out_shape=[jax.ShapeDtypeStruct((m, D_MODEL), F32)] * 2,
        compiler_params=_params(1),
        name="short_conv_tok",
    )(x, g_pre, win_bf16, conv_w, wout_bf16, g_post, inj1, inj2)
    return out, u.reshape(nb, seq, D_MODEL)[:, seq - (CONV_WIDTH - 1):]


def kernel(x_prompt, x_sample, cache_k, cache_v, state_conv, page_table, rel_bias, w_qkv, w_o, lambda_q1, lambda_k1, lambda_q2, lambda_k2, subln_g, w_conv_in, conv_w, w_conv_out, norm_mix_pre, norm_mix_post, norm_ffn_pre, norm_ffn_post, w_ffn_gu, w_ffn_down):
    b, s, d = x_prompt.shape
    nb, dec_seq, _ = x_sample.shape
    assert d == D_MODEL and page_table.shape[1] * PAGE_SIZE >= FAR_DISTANCE
    mp, ms = b * s, nb * dec_seq
    tm_p = TOKEN_TILE

    wqkv = w_qkv.astype(BF16)
    wo = w_o.astype(BF16)
    win = w_conv_in.astype(BF16)
    wout = w_conv_out.astype(BF16)
    gain =lambda g, i: g[i].reshape(1, D_MODEL)
    lam_vecs = jnp.stack([lambda_q1, lambda_k1, lambda_q2, lambda_k2])
    lam0 = _lambda_init(0)

    xp = x_prompt.reshape(mp, d)
    xs = x_sample.reshape(ms, d)

    g_pre = gain(norm_mix_pre, 0)
    kf_p, vf_p, kb_p, qt_p, vt_p = _qkv_call(xp, g_pre, wqkv, tm_p, s)
    kf_s, vf_s, kb_s, qt_s, _ = _qkv_call(xs, g_pre, wqkv, ms, ms)

    q4 = qt_s.reshape(N_HEADS, QK_DIM, nb, dec_seq).transpose(2, 0, 3, 1)
    q_rows = jnp.stack([q4, q4], axis=2).reshape(nb, N_HEADS * 2 * dec_seq, QK_DIM)
    pad_tok = PAGE_SIZE // N_HEADS - dec_seq
    flat_new = lambda a: jnp.pad(a.reshape(nb, dec_seq, N_HEADS, QK_DIM),
                                 ((0, 0), (0, pad_tok), (0, 0), (0, 0))).reshape(nb, PAGE_SIZE, QK_DIM)
    on_p, on_rows = _attn_call(qt_p, kb_p, vt_p, q_rows, flat_new(kb_s[0].transpose(1, 0, 2)),
                               flat_new(vf_s.astype(BF16)), cache_k, cache_v, page_table, rel_bias,
                               lam_vecs, subln_g, lam0, dec_seq, ATTN_QUERY_TILE, ATTN_KEY_TILE)
    on_s = on_rows.reshape(nb, N_HEADS, 2, dec_seq, V_DIM)[:, :, 0].transpose(1, 0, 2, 3)
    on_s = on_s.reshape(1, N_HEADS, ms, V_DIM).astype(BF16)

    g_post = gain(norm_mix_post, 0)

    def ffn_both(xs_, xp_, i, proj_s=None, proj_p=None):
        xs_, wg, wu, wd = _ffn_stream_call(xs_, gain(norm_ffn_pre, i), w_ffn_gu, w_ffn_down, i,
                                           gain(norm_ffn_post, i), proj_s)
        xp_ = _ffn_call(xp_, gain(norm_ffn_pre, i), wg, wu, wd, gain(norm_ffn_post, i), tm_p, proj_p)
        return xs_, xp_

    xs, xp = ffn_both(xs, xp, 0, (on_s, wo, g_post), (on_p, wo, g_post))

    g_pre, g_post = gain(norm_mix_pre, 1), gain(norm_mix_post, 1)
    zero_state = jnp.zeros((b, CONV_WIDTH - 1, d), F32)
    xp, conv_p = _conv_call(xp, g_pre, win, conv_w, wout, g_post, zero_state, tm_p, s)
    xs, conv_s = _conv_call(xs, g_pre, win, conv_w, wout, g_post, state_conv, ms, dec_seq)
    xs, xp = ffn_both(xs, xp, 1)

    return (xp.reshape(b, s, d), xs.reshape(nb, dec_seq, d),
            kf_p.reshape(b, s, N_HEADS, QK_DIM), vf_p.reshape(b, s, N_HEADS, V_DIM),
            kf_s.reshape(nb, dec_seq, N_HEADS, QK_DIM), vf_s.reshape(nb, dec_seq, N_HEADS, V_DIM),
            conv_p, conv_s)
```

```python
import functools
import math

import jax
import jax.numpy as jnp
from jax import lax
from jax.experimental import pallas as pl
from jax.experimental.pallas import tpu as pltpu

D_MODEL = 1024
N_HEADS = 8
HEAD_DIM = 64
QK_DIM = 2 * HEAD_DIM
V_DIM = 2 * HEAD_DIM
ATTN_WIDTH = N_HEADS * V_DIM
N_BUCKETS = 32
MAX_DISTANCE = 128
CONV_WIDTH = 3
PAGE_SIZE = 128
EPS = 1e-6
NEG_INF = -1e30
SCALE = HEAD_DIM ** -0.5
LOG2E = math.log2(math.e)

V7X_LANES = 128
V7X_BF16_SUBLANES = 16
V_EXT = V_DIM + V7X_BF16_SUBLANES
V7X_SUBLANES = 8
VMEM_LIMIT_BYTES = 56 * 1024 * 1024
SAMPLE_PAGES_PER_UPDATE = 8
ATTN_VMEM_LIMIT_BYTES = 60 * 1024 * 1024
FFN_STREAM_CHUNK = 256
TOKEN_TILE = 512
ATTN_QUERY_TILE = 512
ATTN_KEY_TILE = 256
FAR_TRIP_PAIRS = (2, 1)
F32 = jnp.float32
BF16 = jnp.bfloat16


def _lambda_init(layer):
    return 0.8 - 0.6 * math.exp(-0.3 * layer)


def _bucket_uppers():
    max_exact = N_BUCKETS // 2
    n_far = None
    buckets = []
    for n in range(0, 4 * MAX_DISTANCE):
        if n < max_exact:
            b = n
        else:
            r = math.log(n / max_exact) / math.log(MAX_DISTANCE / max_exact) * (N_BUCKETS - max_exact)
            frac = r - math.floor(r)
            assert min(frac, 1 - frac) > 1e-4 or n == max_exact or n >= MAX_DISTANCE, n
            b = min(max_exact + int(r + (1e-6 if n == max_exact else 0.0)), N_BUCKETS - 1)
            if n >= MAX_DISTANCE:
                b = N_BUCKETS - 1
        buckets.append(b)
    uppers = []
    for b in range(N_BUCKETS - 1):
        uppers.append(next(n for n, bb in enumerate(buckets) if bb > b))
    n_far = uppers[-1]
    return tuple(uppers), n_far


BUCKET_UPPERS, FAR_DISTANCE = _bucket_uppers()


def _shifted_bias(n, relb_ref, head):
    far = relb_ref[N_BUCKETS - 1, head]
    val = jnp.zeros(n.shape, F32)
    for b in range(N_BUCKETS - 2, -1, -1):
        val = jnp.where(n < BUCKET_UPPERS[b], (relb_ref[b, head] - far) * LOG2E, val)
    return val


def _diff_lambda(lam_ref, lam_init):
    lv = lam_ref[...]
    a = jnp.sum(lv[0:1] * lv[1:2], axis=1, keepdims=True)
    b = jnp.sum(lv[2:3] * lv[3:4], axis=1, keepdims=True)
    return jnp.exp(a) - jnp.exp(b) + lam_init


def _rms(x, g):
    return x * lax.rsqrt(jnp.mean(x * x, axis=-1, keepdims=True) + EPS) * g


def _head_norm(o, g, lam_init):
    return o * lax.rsqrt(jnp.mean(o * o, axis=-1, keepdims=True) + EPS) * g * (1.0 - lam_init)


def _const_spec(shape):
    return pl.BlockSpec(shape, lambda *_: (0,) * len(shape), pipeline_mode=pl.Buffered(1))


def _params(n_grid):
    return pltpu.CompilerParams(dimension_semantics=("arbitrary",) * n_grid,
                                vmem_limit_bytes=VMEM_LIMIT_BYTES)


def _qkv_kernel(x_ref, g_ref, w_ref, kf_ref, vf_ref, kb_ref, qt_ref, vt_ref):
    hb = _rms(x_ref[...], g_ref[...]).astype(BF16)
    w = ATTN_WIDTH
    tm = hb.shape[0]
    q = jnp.dot(hb, w_ref[:, 0:w], preferred_element_type=F32)
    qt_ref[...] = (q * (SCALE * LOG2E)).T.astype(BF16).reshape(N_HEADS, QK_DIM, tm)
    k = jnp.dot(hb, w_ref[:, w:2 * w], preferred_element_type=F32)
    kf_ref[...] = k
    for h in range(N_HEADS):
        kb_ref[h] = k[:, h * QK_DIM:(h + 1) * QK_DIM].astype(BF16)
    v = jnp.dot(hb, w_ref[:, 2 * w:3 * w], preferred_element_type=F32)
    vf_ref[...] = v
    vt_ref[:, 0:V_DIM, :] = v.T.astype(BF16).reshape(N_HEADS, V_DIM, tm)
    vt_ref[:, V_DIM:V_EXT, :] = jnp.ones((N_HEADS, V_EXT - V_DIM, tm), BF16)


def _qkv_call(x, g, w_bf16, tm, seq):
    m = x.shape[0]
    tiles_per_seq = seq // tm
    row = lambda width: pl.BlockSpec((tm, width), lambda i: (i, 0))
    fmaj = lambda rows: pl.BlockSpec((None, N_HEADS, rows, tm),
                                     lambda i: (i // tiles_per_seq, 0, 0, i % tiles_per_seq))
    hmaj = pl.BlockSpec((None, N_HEADS, tm, QK_DIM),
                        lambda i: (i // tiles_per_seq, 0, i % tiles_per_seq, 0))
    return pl.pallas_call(
        _qkv_kernel,
        grid=(m // tm,),
        in_specs=[row(D_MODEL), _const_spec((1, D_MODEL)), _const_spec(w_bf16.shape)],
        out_specs=[row(ATTN_WIDTH)] * 2 + [hmaj, fmaj(QK_DIM), fmaj(V_EXT)],
        out_shape=[jax.ShapeDtypeStruct((m, ATTN_WIDTH), F32)] * 2
        + [jax.ShapeDtypeStruct((m // seq, N_HEADS, seq, QK_DIM), BF16),
           jax.ShapeDtypeStruct((m // seq, N_HEADS, QK_DIM, seq), BF16),
           jax.ShapeDtypeStruct((m // seq, N_HEADS, V_EXT, seq), BF16)],
        compiler_params=_params(1),
        name="qkv_proj",
    )(x, g, w_bf16)


def _prompt_tile(relb_ref, lam_ref, g_ref, qt_ref, k_ref, vt_ref, o_ref,
                 q2_ref, m_ref, acc_ref, bias_ref, s0_ref, s1_ref, p0_ref, p1_ref,
                 a0_ref, a1_ref, t0_ref, t1_ref, *, tq, tk, lam_init, head, qi, before_drain):
    chains = range(qt_ref.shape[0])
    ratio = tq // tk
    n_band = ratio + 1
    last = ratio * (qi + 1) - 1

    @pl.when(qi == 0)
    def _build_bias():
        j = lax.broadcasted_iota(jnp.int32, (tk, tq), 0)
        i = lax.broadcasted_iota(jnp.int32, (tk, tq), 1)
        for d in range(n_band):
            n = i - j + tk * (d + 1 - ratio)
            bias_ref[d] = jnp.where(n >= 0, _shifted_bias(n, relb_ref, head), NEG_INF)
        bias_ref[n_band] = jnp.zeros((tk, tq), F32)

    feat = lax.broadcasted_iota(jnp.int32, (QK_DIM, tq), 0)
    for c in chains:
        qt = qt_ref[c]
        zero = jnp.zeros_like(qt)
        q2_ref[c, :, 0:tq] = jnp.where(feat < HEAD_DIM, qt, zero)
        q2_ref[c, :, tq:2 * tq] = jnp.where(feat >= HEAD_DIM, qt, zero)
    m_ref[...] = jnp.full(m_ref.shape, NEG_INF, F32)
    acc_ref[...] = jnp.zeros(acc_ref.shape, F32)
    for p_buf, a_buf in ((p0_ref, a0_ref), (p1_ref, a1_ref)):
        p_buf[...] = jnp.zeros(p_buf.shape, BF16)
        a_buf[...] = jnp.ones(a_buf.shape, F32)

    @pl.when((head == 0) & (qi == 0))
    def _first_touch():
        for s_buf, t_buf in ((s0_ref, t0_ref), (s1_ref, t1_ref)):
            s_buf[...] = jnp.zeros(s_buf.shape, F32)
            t_buf[...] = jnp.zeros(t_buf.shape, F32)

    def time_step(tau, parity, *, scores=True, softmax=True, values=True, with_bias=False):
        s_buf, t_buf = (s0_ref, t0_ref) if parity == 0 else (s1_ref, t1_ref)
        p_buf, a_buf = (p0_ref, a0_ref) if parity == 0 else (p1_ref, a1_ref)
        jv = pl.multiple_of(jnp.clip(tau - 4, 0, last) * tk, tk)
        valid = (tau >= 2) & (tau - 2 <= last)
        jq = jnp.minimum(tau, last)
        start = pl.multiple_of(jq * tk, tk)
        if scores and with_bias:
            b = bias_ref[jnp.minimum(last - jq, n_band)]
            b2 = jnp.concatenate([b, b], axis=1)

        def values_stage(c):
            pv = jnp.dot(vt_ref[c, :, pl.ds(jv, tk)], p_buf[c], preferred_element_type=F32)
            acc_ref[c] = a_buf[c] * acc_ref[c] + pv

        def softmax_stage(c):
            m_prev = m_ref[c]
            m_new = jnp.where(valid, jnp.maximum(m_prev, t_buf[c]), m_prev)
            a_buf[c] = jnp.exp2(m_prev - m_new)
            p_buf[c] = jnp.exp2((s_buf[c] - jnp.where(valid, m_new, -NEG_INF)).astype(BF16))
            m_ref[c] = m_new

        def scores_stage(c):
            s_new = jnp.dot(k_ref[c, pl.ds(start, tk), :], q2_ref[c],
                            preferred_element_type=F32)
            if with_bias:
                s_new = s_new + b2
            s_buf[c] = s_new
            t_buf[c] = jnp.max(s_new, axis=0, keepdims=True)

        for c in chains:
            if values:
                values_stage(c)
            if softmax:
                softmax_stage(c)
            if scores:
                scores_stage(c)

    n_far = jnp.maximum(last + 1 - n_band, 0)
    n_pairs = n_far // 2
    fill_pairs = FAR_TRIP_PAIRS[0]
    assert 2 * fill_pairs == 4
    filled = n_pairs >= fill_pairs

    @pl.when(filled)
    def _fill():
        for d in range(2 * fill_pairs):
            time_step(d, d % 2, values=False, softmax=d >= 2)

    done = filled.astype(jnp.int32) * fill_pairs
    for pairs_per_trip in FAR_TRIP_PAIRS:
        def far_trip(u, carry, pairs_per_trip=pairs_per_trip):
            for d in range(2 * pairs_per_trip):
                time_step(2 * u * pairs_per_trip + d, d % 2)
            return carry

        trips = n_pairs // pairs_per_trip
        lax.fori_loop(done // pairs_per_trip, trips, far_trip, 0)
        done = trips * pairs_per_trip
    tau0 = 2 * n_pairs
    n_tail = n_band + 1
    before_drain()
    for d in range(n_tail + 4):
        time_step(tau0 + d, d % 2, scores=d < n_tail, softmax=d < n_tail + 2, with_bias=True)

    lam = _diff_lambda(lam_ref, lam_init)
    gain = g_ref[...] * (1.0 - lam_init)
    for c in chains:
        inv_l = 1.0 / acc_ref[c, V_DIM:V_DIM + 1, :]
        o1 = acc_ref[c, 0:V_DIM, 0:tq] * inv_l[:, 0:tq]
        o2 = acc_ref[c, 0:V_DIM, tq:2 * tq] * inv_l[:, tq:2 * tq]
        ot = o1 - lam * o2
        ms = jnp.mean(ot * ot, axis=0, keepdims=True)
        o_ref[c] = (ot * lax.rsqrt(ms + EPS) * gain).T.astype(BF16)


def _decode_stages(relb_ref, lam_ref, g_ref, q_ref, kn_ref, vn_ref, kc_refs, vc_refs, o_ref,
                   m_ref, l_ref, acc_ref, qm_ref, pageb_ref, newb_ref,
                   *, dec_seq, lam_init, pages_per_update):
    group = len(kc_refs)
    rows = N_HEADS * 2 * dec_seq
    per_head = 2 * dec_seq
    assert per_head == V7X_SUBLANES
    flat = PAGE_SIZE * N_HEADS
    nt = (((1,), (1,)), ((), ()))

    def grids(width):
        r = lax.broadcasted_iota(jnp.int32, (rows, width), 0)
        c = lax.broadcasted_iota(jnp.int32, (rows, width), 1)
        same_head = (c % N_HEADS) == (r // per_head)
        return r % dec_seq, c // N_HEADS, same_head

    def head_bias(dist):
        parts = [_shifted_bias(dist[h * per_head:(h + 1) * per_head], relb_ref, h)
                 for h in range(N_HEADS)]
        return jnp.concatenate(parts, axis=0)

    def begin(build_tables, first):
        @pl.when(build_tables)
        def _tables():
            tq, tk, ok = grids(flat)
            pageb_ref[0] = jnp.where(ok, 0.0, NEG_INF)
            pageb_ref[1] = jnp.where(ok, head_bias(PAGE_SIZE + tq - tk), NEG_INF)
            tqn, tkn, okn = grids(kn_ref.shape[0])
            newb_ref[...] = jnp.where(okn & (tkn <= tqn), head_bias(tqn - tkn), NEG_INF)

        @pl.when(first)
        def _init():
            m_ref[...] = jnp.full(m_ref.shape, NEG_INF, F32)
            l_ref[...] = jnp.zeros(l_ref.shape, F32)
            acc_ref[...] = jnp.zeros(acc_ref.shape, F32)
            q = q_ref[...]
            row = lax.broadcasted_iota(jnp.int32, (rows, QK_DIM), 0)
            lane = lax.broadcasted_iota(jnp.int32, (rows, QK_DIM), 1)
            first_comp = (row // dec_seq) % 2 == 0
            qm_ref[...] = jnp.where((lane < HEAD_DIM) == first_comp, q, jnp.zeros_like(q))

    def update(s_list, v_list):
        m_prev = m_ref[...]
        m_tile = functools.reduce(jnp.maximum, [jnp.max(s, axis=1, keepdims=True) for s in s_list])
        m_new = jnp.maximum(m_prev, m_tile)
        alpha = jnp.exp2(m_prev - m_new)
        p_list = [jnp.exp2(s - m_new[:, 0:1]) for s in s_list]
        l_ref[...] = alpha * l_ref[...] + sum(jnp.sum(p, axis=1, keepdims=True) for p in p_list)
        pv = sum(jnp.dot(p.astype(BF16), v(), preferred_element_type=F32)
                 for p, v in zip(p_list, v_list))
        acc_ref[...] = alpha * acc_ref[...] + pv
        m_ref[...] = m_new

    def scores(keys, bias):
        return lax.dot_general(qm_ref[...], keys, nt, preferred_element_type=F32) + bias

    def scores_t(keys_t, bias):
        return jnp.dot(qm_ref[...], keys_t, preferred_element_type=F32) + bias

    def pages(last):
        for lo in range(0, group, pages_per_update):
            hi = lo + pages_per_update
            slots = [0] * (hi - lo - 1) + [last.astype(jnp.int32) if hi == group else 0]
            update([scores_t(kc[...].reshape(flat, QK_DIM).T.astype(BF16), pageb_ref[slot])
                    for kc, slot in zip(kc_refs[lo:hi], slots)],
                   [lambda vc=vc: vc[...].reshape(flat, V_DIM).astype(BF16) for vc in vc_refs[lo:hi]])

    def end(last):
        @pl.when(last)
        def _finish():
            update([scores(kn_ref[...], newb_ref[...])], [lambda: vn_ref[...]])
            o = acc_ref[...] / l_ref[...]
            o2 = pltpu.roll(o, rows - dec_seq, axis=0)
            od = o - _diff_lambda(lam_ref, lam_init) * o2
            o_ref[...] = _head_norm(od, g_ref[...], lam_init)

    return begin, pages, end


N_PROMPT_SCRATCH = 12
N_DECODE_SCRATCH = 6


def _attn_kernel(pt_ref, relb_ref, lam_ref, gcol_ref, grow_ref, qt_ref, k_ref, vt_ref,
                 qs_ref, kn_ref, vn_ref, *rest, tq, tk, lam_init, dec_seq, pages, steps_per_row):
    del pt_ref
    kc_refs, vc_refs = rest[:pages], rest[pages:2 * pages]
    op_ref, os_ref = rest[2 * pages:2 * pages + 2]
    scratch = rest[2 * pages + 2:]
    prompt_scratch, decode_scratch = scratch[:N_PROMPT_SCRATCH], scratch[N_PROMPT_SCRATCH:]
    head, qi = pl.program_id(0), pl.program_id(1)
    step = head * pl.num_programs(1) + qi
    sub = step % steps_per_row
    last = sub == steps_per_row - 1
    decode_begin, decode_pages, decode_end = _decode_stages(
        relb_ref, lam_ref, grow_ref, qs_ref, kn_ref, vn_ref, kc_refs, vc_refs, os_ref, *decode_scratch,
        dec_seq=dec_seq, lam_init=lam_init, pages_per_update=SAMPLE_PAGES_PER_UPDATE)
    decode_begin(step == 0, sub == 0)
    _prompt_tile(relb_ref, lam_ref, gcol_ref, qt_ref, k_ref, vt_ref, op_ref, *prompt_scratch,
                 tq=tq, tk=tk, lam_init=lam_init, head=head, qi=qi,
                 before_drain=lambda: decode_pages(last))
    decode_end(last)


def _attn_call(qt, kb, vt, q_rows, k_new, v_new, cache_k, cache_v, page_table, rel_bias, lam_vecs,
               subln_g, lam_init, dec_seq, tq, tk):
    b, _, _, s = qt.shape
    nb, rows, _ = q_rows.shape
    n_pages = page_table.shape[1]
    n_steps = N_HEADS * (s // tq)
    assert s % tq == 0 and tq % tk == 0 and tk >= FAR_DISTANCE and PAGE_SIZE >= FAR_DISTANCE + dec_seq
    assert (nb * n_pages) % n_steps == 0
    pages = nb * n_pages // n_steps
    assert n_pages % pages == 0 and pages % SAMPLE_PAGES_PER_UPDATE == 0
    steps_per_row = n_pages // pages
    flat = PAGE_SIZE * N_HEADS
    n_q = s // tq

    def row_of(h, qi):
        step = h * n_q + qi
        return step // steps_per_row, (step % steps_per_row) * pages

    const = lambda shape: pl.BlockSpec(shape, lambda h, qi, pt: (0,) * len(shape),
                                       pipeline_mode=pl.Buffered(1))
    per_row = lambda shape: pl.BlockSpec((None,) + shape, lambda h, qi, pt: (row_of(h, qi)[0], 0, 0))
    paged = lambda g: pl.BlockSpec(
        (None, PAGE_SIZE, N_HEADS, QK_DIM),
        lambda h, qi, pt: (pt[row_of(h, qi)[0], row_of(h, qi)[1] + g], 0, 0, 0))
    out_tile = pl.BlockSpec((b, None, tq, V_DIM), lambda h, qi, pt: (0, h, qi, 0))
    q_tile = pl.BlockSpec((b, None, QK_DIM, tq), lambda h, qi, pt: (0, h, 0, qi))
    k_seq = pl.BlockSpec((b, None, s, QK_DIM), lambda h, qi, pt: (0, h, 0, 0), pipeline_mode=pl.Buffered(1))
    v_seq = pl.BlockSpec((b, None, V_EXT, s), lambda h, qi, pt: (0, h, 0, 0), pipeline_mode=pl.Buffered(1))
    stat = pltpu.VMEM((b, 1, 2 * tq), F32)
    prompt_scratch = [pltpu.VMEM((b, QK_DIM, 2 * tq), BF16), stat,
                      pltpu.VMEM((b, V_EXT, 2 * tq), F32),
                      pltpu.VMEM((tq // tk + 2, tk, tq), F32),
                      pltpu.VMEM((b, tk, 2 * tq), F32), pltpu.VMEM((b, tk, 2 * tq), F32),
                      pltpu.VMEM((b, tk, 2 * tq), BF16), pltpu.VMEM((b, tk, 2 * tq), BF16),
                      stat, stat, stat, stat]
    decode_scratch = [pltpu.VMEM((rows, V7X_LANES), F32),
                      pltpu.VMEM((rows, V7X_LANES), F32),
                      pltpu.VMEM((rows, V_DIM), F32),
                      pltpu.VMEM((rows, QK_DIM), BF16),
                      pltpu.VMEM((2, rows, flat), F32),
                      pltpu.VMEM((rows, k_new.shape[1]), F32)]
    assert len(prompt_scratch) == N_PROMPT_SCRATCH and len(decode_scratch) == N_DECODE_SCRATCH
    grid_spec = pltpu.PrefetchScalarGridSpec(
        num_scalar_prefetch=1,
        grid=(N_HEADS, n_q),
        in_specs=[pl.BlockSpec(memory_space=pltpu.SMEM), const((4, HEAD_DIM)), const((V_DIM, 1)),
                  const((1, V_DIM)), q_tile, k_seq, v_seq,
                  per_row((rows, QK_DIM)), per_row(k_new.shape[1:]), per_row(v_new.shape[1:])]
        + [paged(g) for g in range(pages)] * 2,
        out_specs=[out_tile, per_row((rows, V_DIM))],
        scratch_shapes=prompt_scratch + decode_scratch,
    )
    return pl.pallas_call(
        functools.partial(_attn_kernel, tq=tq, tk=tk, lam_init=lam_init, dec_seq=dec_seq, pages=pages,
                          steps_per_row=steps_per_row),
        grid_spec=grid_spec,
        out_shape=[jax.ShapeDtypeStruct((b, N_HEADS, s, V_DIM), BF16),
                   jax.ShapeDtypeStruct((nb, rows, V_DIM), F32)],
        compiler_params=pltpu.CompilerParams(dimension_semantics=("arbitrary",) * 2,
                                             vmem_limit_bytes=ATTN_VMEM_LIMIT_BYTES),
        name="attention",
    )(page_table, rel_bias, lam_vecs, subln_g.reshape(V_DIM, 1), subln_g.reshape(1, V_DIM), qt, kb, vt,
      q_rows, k_new, v_new, *([cache_k] * pages), *([cache_v] * pages))


def _silu_mul(g, u):
    return (g * (1.0 / (1.0 + jnp.exp(-g))) * u).astype(BF16)


def _with_mixer_branch(x_ref, proj_refs):
    if not proj_refs:
        return x_ref[...]
    a_ref, wo_ref, gmix_ref = proj_refs
    a = jnp.concatenate([a_ref[h] for h in range(N_HEADS)], axis=1)
    return x_ref[...] + _rms(jnp.dot(a, wo_ref[...], preferred_element_type=F32), gmix_ref[...])


def _ffn_kernel(*refs, with_proj):
    x_ref, proj_refs = refs[0], (refs[1:4] if with_proj else ())
    gpre_ref, wg_ref, wu_ref, wd_ref, gpost_ref, o_ref = refs[1 + len(proj_refs):]
    x = _with_mixer_branch(x_ref, proj_refs)
    hb = _rms(x, gpre_ref[...]).astype(BF16)
    g = jnp.dot(hb, wg_ref[...], preferred_element_type=F32)
    u = jnp.dot(hb, wu_ref[...], preferred_element_type=F32)
    y = jnp.dot(_silu_mul(g, u), wd_ref[...], preferred_element_type=F32)
    o_ref[...] = x + _rms(y, gpost_ref[...])


def _ffn_stream_kernel(*refs, with_proj):
    x_ref, proj_refs = refs[0], (refs[1:4] if with_proj else ())
    (gpre_ref, wg_ref, wu_ref, wd_ref, gpost_ref,
     o_ref, wgb_ref, wub_ref, wdb_ref, x_acc_ref, hb_ref, y_ref) = refs[1 + len(proj_refs):]
    c = pl.program_id(0)

    @pl.when(c == 0)
    def _first_chunk():
        x = _with_mixer_branch(x_ref, proj_refs)
        x_acc_ref[...] = x
        hb_ref[...] = _rms(x, gpre_ref[...]).astype(BF16)
        y_ref[...] = jnp.zeros(y_ref.shape, F32)

    wg, wu, wd = wg_ref[...].astype(BF16), wu_ref[...].astype(BF16), wd_ref[...].astype(BF16)
    wgb_ref[...], wub_ref[...], wdb_ref[...] = wg, wu, wd
    hb = hb_ref[...]
    g = jnp.dot(hb, wg, preferred_element_type=F32)
    u = jnp.dot(hb, wu, preferred_element_type=F32)
    y_ref[...] += jnp.dot(_silu_mul(g, u), wd, preferred_element_type=F32)

    @pl.when(c == pl.num_programs(0) - 1)
    def _last_chunk():
        o_ref[...] = x_acc_ref[...] + _rms(y_ref[...], gpost_ref[...])


def _proj_specs(proj, tm, tile_of):
    a_bf16, wo_bf16, g_mix = proj
    tiles_per_seq = a_bf16.shape[2] // tm
    heads = pl.BlockSpec((None, N_HEADS, tm, V_DIM),
                         lambda i: (tile_of(i) // tiles_per_seq, 0, tile_of(i) % tiles_per_seq, 0))
    return [heads, _const_spec(wo_bf16.shape), _const_spec((1, D_MODEL))], (a_bf16, wo_bf16, g_mix)


def _ffn_call(x, g_pre, wg_bf16, wu_bf16, wd_bf16, g_post, tm, proj=None):
    m = x.shape[0]
    row = pl.BlockSpec((tm, D_MODEL), lambda i: (i, 0))
    proj_specs, proj_args = _proj_specs(proj, tm, lambda i: i) if proj else ([], ())
    specs = [row] + proj_specs + [_const_spec((1, D_MODEL)), _const_spec(wg_bf16.shape),
                                  _const_spec(wu_bf16.shape), _const_spec(wd_bf16.shape),
                                  _const_spec((1, D_MODEL))]
    return pl.pallas_call(
        functools.partial(_ffn_kernel, with_proj=proj is not None),
        grid=(m // tm,),
        in_specs=specs,
        out_specs=row,
        out_shape=jax.ShapeDtypeStruct((m, D_MODEL), F32),
        compiler_params=_params(1),
        name="swiglu_ffn",
    )(x, *proj_args, g_pre, wg_bf16, wu_bf16, wd_bf16, g_post)


def _ffn_stream_call(x, g_pre, w_gu, w_down, layer, g_post, proj=None):
    m = x.shape[0]
    d_ff = w_down.shape[1]
    tc = FFN_STREAM_CHUNK
    n_chunks = d_ff // tc
    assert d_ff % tc == 0
    whole = pl.BlockSpec((m, D_MODEL), lambda c: (0, 0))
    proj_specs, proj_args = _proj_specs(proj, m, lambda c: 0) if proj else ([], ())
    specs = [whole] + proj_specs + [
        _const_spec((1, D_MODEL)),
        pl.BlockSpec((None, D_MODEL, tc), lambda c: (layer, 0, c)),
        pl.BlockSpec((None, D_MODEL, tc), lambda c: (layer, 0, n_chunks + c)),
        pl.BlockSpec((None, tc, D_MODEL), lambda c: (layer, c, 0)),
        _const_spec((1, D_MODEL))]
    cols = pl.BlockSpec((D_MODEL, tc), lambda c: (0, c))
    return pl.pallas_call(
        functools.partial(_ffn_stream_kernel, with_proj=proj is not None),
        grid=(n_chunks,),
        in_specs=specs,
        out_specs=[whole, cols, cols, pl.BlockSpec((tc, D_MODEL), lambda c: (c, 0))],
        out_shape=[jax.ShapeDtypeStruct((m, D_MODEL), F32),
                   jax.ShapeDtypeStruct((D_MODEL, d_ff), BF16),
                   jax.ShapeDtypeStruct((D_MODEL, d_ff), BF16),
                   jax.ShapeDtypeStruct((d_ff, D_MODEL), BF16)],
        scratch_shapes=[pltpu.VMEM((m, D_MODEL), F32), pltpu.VMEM((m, D_MODEL), BF16),
                        pltpu.VMEM((m, D_MODEL), F32)],
        compiler_params=_params(1),
        name="swiglu_ffn_stream",
    )(x, *proj_args, g_pre, w_gu, w_gu, w_down, g_post)


def _conv_kernel(*refs, tm, seq):
    carry_mode = seq >= tm
    if carry_mode:
        (x_ref, gpre_ref, win_ref, cw_ref, wout_ref, gpost_ref, st_ref,
         o_ref, tail_ref, carry_ref) = refs
    else:
        (x_ref, gpre_ref, win_ref, cw_ref, wout_ref, gpost_ref, inj1_ref, inj2_ref,
         o_ref, u_ref) = refs
    d = D_MODEL
    x = x_ref[...]
    hb = _rms(x, gpre_ref[...]).astype(BF16)
    gc = jnp.dot(hb, win_ref[:, d:2 * d], preferred_element_type=F32)
    xt = jnp.dot(hb, win_ref[:, 2 * d:3 * d], preferred_element_type=F32)
    u = gc * xt
    row = lax.broadcasted_iota(jnp.int32, (tm, d), 0)
    r1 = pltpu.roll(u, 1, axis=0)
    r2 = pltpu.roll(u, 2, axis=0)
    if carry_mode:
        i = pl.program_id(0)
        tiles_per_seq = seq // tm

        @pl.when(i % tiles_per_seq == 0)
        def _seq_start():
            carry_ref[...] = st_ref[...]

        c = carry_ref[...]
        c1 = c[V7X_SUBLANES - 1:V7X_SUBLANES]
        c2 = c[V7X_SUBLANES - 2:V7X_SUBLANES - 1]
        prev1 = jnp.where(row == 0, c1, r1)
        prev2 = jnp.where(row == 0, c2, jnp.where(row == 1, c1, r2))
        tail = u[tm - V7X_SUBLANES:tm]
        carry_ref[...] = tail
        tail_ref[...] = tail
    else:
        t = row % seq
        prev1 = jnp.where(t >= 1, r1, inj1_ref[...])
        prev2 = jnp.where(t >= 2, r2, inj2_ref[...])
        u_ref[...] = u
    cw = cw_ref[...]
    y = cw[0:1] * prev2 + cw[1:2] * prev1 + cw[2:3] * u
    gb = jnp.dot(hb, win_ref[:, 0:d], preferred_element_type=F32)
    z = jnp.dot((gb * y).astype(BF16), wout_ref[...], preferred_element_type=F32)
    o_ref[...] = x + _rms(z, gpost_ref[...])


def _conv_call(x, g_pre, win_bf16, conv_w, wout_bf16, g_post, state, tm, seq):
    m = x.shape[0]
    nb = m // seq
    row = pl.BlockSpec((tm, D_MODEL), lambda i: (i, 0))
    common = [row, _const_spec((1, D_MODEL)), _const_spec(win_bf16.shape),
              _const_spec((CONV_WIDTH, D_MODEL)), _const_spec(wout_bf16.shape),
              _const_spec((1, D_MODEL))]
    kern = functools.partial(_conv_kernel, tm=tm, seq=seq)
    if seq >= tm:
        assert seq % tm == 0
        n_tiles = m // tm
        tiles_per_seq = seq // tm
        pad = jnp.zeros((nb, V7X_SUBLANES - (CONV_WIDTH - 1), D_MODEL), F32)
        st = jnp.concatenate([pad, state], axis=1)
        st_spec = pl.BlockSpec((None, V7X_SUBLANES, D_MODEL), lambda i: (i // tiles_per_seq, 0, 0))
        tail_spec = pl.BlockSpec((None, V7X_SUBLANES, D_MODEL), lambda i: (i, 0, 0))
        out, tails = pl.pallas_call(
            kern, grid=(n_tiles,),
            in_specs=common + [st_spec],
            out_specs=[row, tail_spec],
            out_shape=[jax.ShapeDtypeStruct((m, D_MODEL), F32),
                       jax.ShapeDtypeStruct((n_tiles, V7X_SUBLANES, D_MODEL), F32)],
            scratch_shapes=[pltpu.VMEM((V7X_SUBLANES, D_MODEL), F32)],
            compiler_params=_params(1),
            name="short_conv_seq",
        )(x, g_pre, win_bf16, conv_w, wout_bf16, g_post, st)
        last = tails.reshape(nb, tiles_per_seq, V7X_SUBLANES, D_MODEL)[:, -1]
        return out, last[:, V7X_SUBLANES - (CONV_WIDTH - 1):]
    assert tm % seq == 0 and m == tm and seq >= CONV_WIDTH - 1
    zeros = lambda n: jnp.zeros((nb, n, D_MODEL), F32)
    inj1 = jnp.concatenate([state[:, 1:2], zeros(seq - 1)], axis=1).reshape(m, D_MODEL)
    inj2 = jnp.concatenate([state[:, 0:2], zeros(seq - 2)], axis=1).reshape(m, D_MODEL)
    out, u = pl.pallas_call(
        kern, grid=(1,),
        in_specs=common + [row, row],
        out_specs=[row, row],
        out_shape=[jax.ShapeDtypeStruct((m, D_MODEL), F32)] * 2,
        compiler_params=_params(1),
        name="short_conv_tok",
    )(x, g_pre, win_bf16, conv_w, wout_bf16, g_post, inj1, inj2)
    return out, u.reshape(nb, seq, D_MODEL)[:, seq - (CONV_WIDTH - 1):]


def kernel(x_prompt, x_sample, cache_k, cache_v, state_conv, page_table, rel_bias, w_qkv, w_o, lambda_q1, lambda_k1, lambda_q2, lambda_k2, subln_g, w_conv_in, conv_w, w_conv_out, norm_mix_pre, norm_mix_post, norm_ffn_pre, norm_ffn_post, w_ffn_gu, w_ffn_down):
    b, s, d = x_prompt.shape
    nb, dec_seq, _ = x_sample.shape
    assert d == D_MODEL and page_table.shape[1] * PAGE_SIZE >= FAR_DISTANCE
    mp, ms = b * s, nb * dec_seq
    tm_p = TOKEN_TILE

    wqkv = w_qkv.astype(BF16)
    wo = w_o.astype(BF16)
    win = w_conv_in.astype(BF16)
    wout = w_conv_out.astype(BF16)
    gain =lambda g, i: g[i].reshape(1, D_MODEL)
    lam_vecs = jnp.stack([lambda_q1, lambda_k1, lambda_q2, lambda_k2])
    lam0 = _lambda_init(0)

    xp = x_prompt.reshape(mp, d)
    xs = x_sample.reshape(ms, d)

    g_pre = gain(norm_mix_pre, 0)
    kf_p, vf_p, kb_p, qt_p, vt_p = _qkv_call(xp, g_pre, wqkv, tm_p, s)
    kf_s, vf_s, kb_s, qt_s, _ = _qkv_call(xs, g_pre, wqkv, ms, ms)

    q4 = qt_s.reshape(N_HEADS, QK_DIM, nb, dec_seq).transpose(2, 0, 3, 1)
    q_rows = jnp.stack([q4, q4], axis=2).reshape(nb, N_HEADS * 2 * dec_seq, QK_DIM)
    pad_tok = PAGE_SIZE // N_HEADS - dec_seq
    flat_new = lambda a: jnp.pad(a.reshape(nb, dec_seq, N_HEADS, QK_DIM),
                                 ((0, 0), (0, pad_tok), (0, 0), (0, 0))).reshape(nb, PAGE_SIZE, QK_DIM)
    on_p, on_rows = _attn_call(qt_p, kb_p, vt_p, q_rows, flat_new(kb_s[0].transpose(1, 0, 2)),
                               flat_new(vf_s.astype(BF16)), cache_k, cache_v, page_table, rel_bias,
                               lam_vecs, subln_g, lam0, dec_seq, ATTN_QUERY_TILE, ATTN_KEY_TILE)
    on_s = on_rows.reshape(nb, N_HEADS, 2, dec_seq, V_DIM)[:, :, 0].transpose(1, 0, 2, 3)
    on_s = on_s.reshape(1, N_HEADS, ms, V_DIM).astype(BF16)

    g_post = gain(norm_mix_post, 0)

    def ffn_both(xs_, xp_, i, proj_s=None, proj_p=None):
        xs_, wg, wu, wd = _ffn_stream_call(xs_, gain(norm_ffn_pre, i), w_ffn_gu, w_ffn_down, i,
                                           gain(norm_ffn_post, i), proj_s)
        xp_ = _ffn_call(xp_, gain(norm_ffn_pre, i), wg, wu, wd, gain(norm_ffn_post, i), tm_p, proj_p)
        return xs_, xp_

    xs, xp = ffn_both(xs, xp, 0, (on_s, wo, g_post), (on_p, wo, g_post))

    g_pre, g_post = gain(norm_mix_pre, 1), gain(norm_mix_post, 1)
    zero_state = jnp.zeros((b, CONV_WIDTH - 1, d), F32)
    xp, conv_p = _conv_call(xp, g_pre, win, conv_w, wout, g_post, zero_state, tm_p, s)
    xs, conv_s = _conv_call(xs, g_pre, win, conv_w, wout, g_post, state_conv, ms, dec_seq)
    xs, xp = ffn_both(xs, xp, 1)

    return (xp.reshape(b, s, d), xs.reshape(nb, dec_seq, d),
            kf_p.reshape(b, s, N_HEADS, QK_DIM), vf_p.reshape(b, s, N_HEADS, V_DIM),
            kf_s.reshape(nb, dec_seq, N_HEADS, QK_DIM), vf_s.reshape(nb, dec_seq, N_HEADS, V_DIM),
            conv_p, conv_s)
```

```python
import functools
import math

import jax
import jax.numpy as jnp
from jax import lax
from jax.experimental import pallas as pl
from jax.experimental.pallas import tpu as pltpu

D_MODEL = 1024
N_HEADS = 8
HEAD_DIM = 64
QK_DIM = 2 * HEAD_DIM
V_DIM = 2 * HEAD_DIM
ATTN_WIDTH = N_HEADS * V_DIM
N_BUCKETS = 32
MAX_DISTANCE = 128
CONV_WIDTH = 3
PAGE_SIZE = 128
EPS = 1e-6
NEG_INF = -1e30
SCALE = HEAD_DIM ** -0.5
LOG2E = math.log2(math.e)

V7X_LANES = 128
V7X_BF16_SUBLANES = 16
V_EXT = V_DIM + V7X_BF16_SUBLANES
V7X_SUBLANES = 8
VMEM_LIMIT_BYTES = 56 * 1024 * 1024
SAMPLE_PAGES_PER_UPDATE = 16
ATTN_VMEM_LIMIT_BYTES = 60 * 1024 * 1024
FFN_STREAM_CHUNK = 256
TOKEN_TILE = 512
ATTN_QUERY_TILE = 512
ATTN_KEY_TILE = 256
FAR_TRIP_PAIRS = (2, 1)
F32 = jnp.float32
BF16 = jnp.bfloat16


def _lambda_init(layer):
    return 0.8 - 0.6 * math.exp(-0.3 * layer)


def _bucket_uppers():
    max_exact = N_BUCKETS // 2
    n_far = None
    buckets = []
    for n in range(0, 4 * MAX_DISTANCE):
        if n < max_exact:
            b = n
        else:
            r = math.log(n / max_exact) / math.log(MAX_DISTANCE / max_exact) * (N_BUCKETS - max_exact)
            frac = r - math.floor(r)
            assert min(frac, 1 - frac) > 1e-4 or n == max_exact or n >= MAX_DISTANCE, n
            b = min(max_exact + int(r + (1e-6 if n == max_exact else 0.0)), N_BUCKETS - 1)
            if n >= MAX_DISTANCE:
                b = N_BUCKETS - 1
        buckets.append(b)
    uppers = []
    for b in range(N_BUCKETS - 1):
        uppers.append(next(n for n, bb in enumerate(buckets) if bb > b))
    n_far = uppers[-1]
    return tuple(uppers), n_far


BUCKET_UPPERS, FAR_DISTANCE = _bucket_uppers()


def _shifted_bias(n, relb_ref, head):
    far = relb_ref[N_BUCKETS - 1, head]
    val = jnp.zeros(n.shape, F32)
    for b in range(N_BUCKETS - 2, -1, -1):
        val = jnp.where(n < BUCKET_UPPERS[b], (relb_ref[b, head] - far) * LOG2E, val)
    return val


def _diff_lambda(lam_ref, lam_init):
    lv = lam_ref[...]
    a = jnp.sum(lv[0:1] * lv[1:2], axis=1, keepdims=True)
    b = jnp.sum(lv[2:3] * lv[3:4], axis=1, keepdims=True)
    return jnp.exp(a) - jnp.exp(b) + lam_init


def _rms(x, g):
    return x * lax.rsqrt(jnp.mean(x * x, axis=-1, keepdims=True) + EPS) * g


def _head_norm(o, g, lam_init):
    return o * lax.rsqrt(jnp.mean(o * o, axis=-1, keepdims=True) + EPS) * g * (1.0 - lam_init)


def _const_spec(shape):
    return pl.BlockSpec(shape, lambda *_: (0,) * len(shape), pipeline_mode=pl.Buffered(1))


def _params(n_grid):
    return pltpu.CompilerParams(dimension_semantics=("arbitrary",) * n_grid,
                                vmem_limit_bytes=VMEM_LIMIT_BYTES)


def _qkv_kernel(x_ref, g_ref, w_ref, kf_ref, vf_ref, kb_ref, qt_ref, vt_ref):
    hb = _rms(x_ref[...], g_ref[...]).astype(BF16)
    w = ATTN_WIDTH
    tm = hb.shape[0]
    q = jnp.dot(hb, w_ref[:, 0:w], preferred_element_type=F32)
    qt_ref[...] = (q * (SCALE * LOG2E)).T.astype(BF16).reshape(N_HEADS, QK_DIM, tm)
    k = jnp.dot(hb, w_ref[:, w:2 * w], preferred_element_type=F32)
    kf_ref[...] = k
    for h in range(N_HEADS):
        kb_ref[h] = k[:, h * QK_DIM:(h + 1) * QK_DIM].astype(BF16)
    v = jnp.dot(hb, w_ref[:, 2 * w:3 * w], preferred_element_type=F32)
    vf_ref[...] = v
    vt_ref[:, 0:V_DIM, :] = v.T.astype(BF16).reshape(N_HEADS, V_DIM, tm)
    vt_ref[:, V_DIM:V_EXT, :] = jnp.ones((N_HEADS, V_EXT - V_DIM, tm), BF16)


def _qkv_call(x, g, w_bf16, tm, seq):
    m = x.shape[0]
    tiles_per_seq = seq // tm
    row = lambda width: pl.BlockSpec((tm, width), lambda i: (i, 0))
    fmaj = lambda rows: pl.BlockSpec((None, N_HEADS, rows, tm),
                                     lambda i: (i // tiles_per_seq, 0, 0, i % tiles_per_seq))
    hmaj = pl.BlockSpec((None, N_HEADS, tm, QK_DIM),
                        lambda i: (i // tiles_per_seq, 0, i % tiles_per_seq, 0))
    return pl.pallas_call(
        _qkv_kernel,
        grid=(m // tm,),
        in_specs=[row(D_MODEL), _const_spec((1, D_MODEL)), _const_spec(w_bf16.shape)],
        out_specs=[row(ATTN_WIDTH)] * 2 + [hmaj, fmaj(QK_DIM), fmaj(V_EXT)],
        out_shape=[jax.ShapeDtypeStruct((m, ATTN_WIDTH), F32)] * 2
        + [jax.ShapeDtypeStruct((m // seq, N_HEADS, seq, QK_DIM), BF16),
           jax.ShapeDtypeStruct((m // seq, N_HEADS, QK_DIM, seq), BF16),
           jax.ShapeDtypeStruct((m // seq, N_HEADS, V_EXT, seq), BF16)],
        compiler_params=_params(1),
        name="qkv_proj",
    )(x, g, w_bf16)


def _prompt_tile(relb_ref, lam_ref, g_ref, qt_ref, k_ref, vt_ref, o_ref,
                 q2_ref, m_ref, acc_ref, bias_ref, s0_ref, s1_ref, p0_ref, p1_ref,
                 a0_ref, a1_ref, t0_ref, t1_ref, *, tq, tk, lam_init, head, qi, before_drain):
    chains = range(qt_ref.shape[0])
    ratio = tq // tk
    n_band = ratio + 1
    last = ratio * (qi + 1) - 1

    @pl.when(qi == 0)
    def _build_bias():
        j = lax.broadcasted_iota(jnp.int32, (tk, tq), 0)
        i = lax.broadcasted_iota(jnp.int32, (tk, tq), 1)
        for d in range(n_band):
            n = i - j + tk * (d + 1 - ratio)
            bias_ref[d] = jnp.where(n >= 0, _shifted_bias(n, relb_ref, head), NEG_INF)
        bias_ref[n_band] = jnp.zeros((tk, tq), F32)

    feat = lax.broadcasted_iota(jnp.int32, (QK_DIM, tq), 0)
    for c in chains:
        qt = qt_ref[c]
        zero = jnp.zeros_like(qt)
        q2_ref[c, :, 0:tq] = jnp.where(feat < HEAD_DIM, qt, zero)
        q2_ref[c, :, tq:2 * tq] = jnp.where(feat >= HEAD_DIM, qt, zero)
    m_ref[...] = jnp.full(m_ref.shape, NEG_INF, F32)
    acc_ref[...] = jnp.zeros(acc_ref.shape, F32)
    for p_buf, a_buf in ((p0_ref, a0_ref), (p1_ref, a1_ref)):
        p_buf[...] = jnp.zeros(p_buf.shape, BF16)
        a_buf[...] = jnp.ones(a_buf.shape, F32)

    @pl.when((head == 0) & (qi == 0))
    def _first_touch():
        for s_buf, t_buf in ((s0_ref, t0_ref), (s1_ref, t1_ref)):
            s_buf[...] = jnp.zeros(s_buf.shape, F32)
            t_buf[...] = jnp.zeros(t_buf.shape, F32)

    def time_step(tau, parity, *, scores=True, softmax=True, values=True, with_bias=False):
        s_buf, t_buf = (s0_ref, t0_ref) if parity == 0 else (s1_ref, t1_ref)
        p_buf, a_buf = (p0_ref, a0_ref) if parity == 0 else (p1_ref, a1_ref)
        jv = pl.multiple_of(jnp.clip(tau - 4, 0, last) * tk, tk)
        valid = (tau >= 2) & (tau - 2 <= last)
        jq = jnp.minimum(tau, last)
        start = pl.multiple_of(jq * tk, tk)
        if scores and with_bias:
            b = bias_ref[jnp.minimum(last - jq, n_band)]
            b2 = jnp.concatenate([b, b], axis=1)

        def values_stage(c):
            pv = jnp.dot(vt_ref[c, :, pl.ds(jv, tk)], p_buf[c], preferred_element_type=F32)
            acc_ref[c] = a_buf[c] * acc_ref[c] + pv

        def softmax_stage(c):
            m_prev = m_ref[c]
            m_new = jnp.where(valid, jnp.maximum(m_prev, t_buf[c]), m_prev)
            a_buf[c] = jnp.exp2(m_prev - m_new)
            p_buf[c] = jnp.exp2((s_buf[c] - jnp.where(valid, m_new, -NEG_INF)).astype(BF16))
            m_ref[c] = m_new

        def scores_stage(c):
            s_new = jnp.dot(k_ref[c, pl.ds(start, tk), :], q2_ref[c],
                            preferred_element_type=F32)
            if with_bias:
                s_new = s_new + b2
            s_buf[c] = s_new
            t_buf[c] = jnp.max(s_new, axis=0, keepdims=True)

        for c in chains:
            if values:
                values_stage(c)
            if softmax:
                softmax_stage(c)
            if scores:
                scores_stage(c)

    n_far = jnp.maximum(last + 1 - n_band, 0)
    n_pairs = n_far // 2
    fill_pairs = FAR_TRIP_PAIRS[0]
    assert 2 * fill_pairs == 4
    filled = n_pairs >= fill_pairs

    @pl.when(filled)
    def _fill():
        for d in range(2 * fill_pairs):
            time_step(d, d % 2, values=False, softmax=d >= 2)

    done = filled.astype(jnp.int32) * fill_pairs
    for pairs_per_trip in FAR_TRIP_PAIRS:
        def far_trip(u, carry, pairs_per_trip=pairs_per_trip):
            for d in range(2 * pairs_per_trip):
                time_step(2 * u * pairs_per_trip + d, d % 2)
            return carry

        trips = n_pairs // pairs_per_trip
        lax.fori_loop(done // pairs_per_trip, trips, far_trip, 0)
        done = trips * pairs_per_trip
    tau0 = 2 * n_pairs
    n_tail = n_band + 1
    before_drain()
    for d in range(n_tail + 4):
        time_step(tau0 + d, d % 2, scores=d < n_tail, softmax=d < n_tail + 2, with_bias=True)

    lam = _diff_lambda(lam_ref, lam_init)
    gain = g_ref[...] * (1.0 - lam_init)
    for c in chains:
        inv_l = 1.0 / acc_ref[c, V_DIM:V_DIM + 1, :]
        o1 = acc_ref[c, 0:V_DIM, 0:tq] * inv_l[:, 0:tq]
        o2 = acc_ref[c, 0:V_DIM, tq:2 * tq] * inv_l[:, tq:2 * tq]
        ot = o1 - lam * o2
        ms = jnp.mean(ot * ot, axis=0, keepdims=True)
        o_ref[c] = (ot * lax.rsqrt(ms + EPS) * gain).T.astype(BF16)


def _decode_stages(relb_ref, lam_ref, g_ref, q_ref, kn_ref, vn_ref, kc_refs, vc_refs, o_ref,
                   m_ref, l_ref, acc_ref, qm_ref, pageb_ref, newb_ref,
                   *, dec_seq, lam_init, pages_per_update):
    group = len(kc_refs)
    rows = N_HEADS * 2 * dec_seq
    per_head = 2 * dec_seq
    assert per_head == V7X_SUBLANES
    flat = PAGE_SIZE * N_HEADS
    nt = (((1,), (1,)), ((), ()))

    def grids(width):
        r = lax.broadcasted_iota(jnp.int32, (rows, width), 0)
        c = lax.broadcasted_iota(jnp.int32, (rows, width), 1)
        same_head = (c % N_HEADS) == (r // per_head)
        return r % dec_seq, c // N_HEADS, same_head

    def head_bias(dist):
        parts = [_shifted_bias(dist[h * per_head:(h + 1) * per_head], relb_ref, h)
                 for h in range(N_HEADS)]
        return jnp.concatenate(parts, axis=0)

    def begin(build_tables, first):
        @pl.when(build_tables)
        def _tables():
            tq, tk, ok = grids(flat)
            pageb_ref[0] = jnp.where(ok, 0.0, NEG_INF)
            pageb_ref[1] = jnp.where(ok, head_bias(PAGE_SIZE + tq - tk), NEG_INF)
            tqn, tkn, okn = grids(kn_ref.shape[0])
            newb_ref[...] = jnp.where(okn & (tkn <= tqn), head_bias(tqn - tkn), NEG_INF)

        @pl.when(first)
        def _init():
            m_ref[...] = jnp.full(m_ref.shape, NEG_INF, F32)
            l_ref[...] = jnp.zeros(l_ref.shape, F32)
            acc_ref[...] = jnp.zeros(acc_ref.shape, F32)
            q = q_ref[...]
            row = lax.broadcasted_iota(jnp.int32, (rows, QK_DIM), 0)
            lane = lax.broadcasted_iota(jnp.int32, (rows, QK_DIM), 1)
            first_comp = (row // dec_seq) % 2 == 0
            qm_ref[...] = jnp.where((lane < HEAD_DIM) == first_comp, q, jnp.zeros_like(q))

    def update(s_list, v_list):
        m_prev = m_ref[...]
        m_tile = functools.reduce(jnp.maximum, [jnp.max(s, axis=1, keepdims=True) for s in s_list])
        m_new = jnp.maximum(m_prev, m_tile)
        alpha = jnp.exp2(m_prev - m_new)
        p_list = [jnp.exp2(s - m_new[:, 0:1]) for s in s_list]
        l_ref[...] = alpha * l_ref[...] + sum(jnp.sum(p, axis=1, keepdims=True) for p in p_list)
        pv = sum(jnp.dot(p.astype(BF16), v(), preferred_element_type=F32)
                 for p, v in zip(p_list, v_list))
        acc_ref[...] = alpha * acc_ref[...] + pv
        m_ref[...] = m_new

    def scores(keys, bias):
        return lax.dot_general(qm_ref[...], keys, nt, preferred_element_type=F32) + bias

    def scores_t(keys_t, bias):
        return jnp.dot(qm_ref[...], keys_t, preferred_element_type=F32) + bias

    def pages(last):
        for lo in range(0, group, pages_per_update):
            hi = lo + pages_per_update
            slots = [0] * (hi - lo - 1) + [last.astype(jnp.int32) if hi == group else 0]
            update([scores_t(kc[...].reshape(flat, QK_DIM).T.astype(BF16), pageb_ref[slot])
                    for kc, slot in zip(kc_refs[lo:hi], slots)],
                   [lambda vc=vc: vc[...].reshape(flat, V_DIM).astype(BF16) for vc in vc_refs[lo:hi]])

    def end(last):
        @pl.when(last)
        def _finish():
            update([scores(kn_ref[...], newb_ref[...])], [lambda: vn_ref[...]])
            o = acc_ref[...] / l_ref[...]
            o2 = pltpu.roll(o, rows - dec_seq, axis=0)
            od = o - _diff_lambda(lam_ref, lam_init) * o2
            o_ref[...] = _head_norm(od, g_ref[...], lam_init)

    return begin, pages, end


N_PROMPT_SCRATCH = 12
N_DECODE_SCRATCH = 6


def _attn_kernel(pt_ref, relb_ref, lam_ref, gcol_ref, grow_ref, qt_ref, k_ref, vt_ref,
                 qs_ref, kn_ref, vn_ref, *rest, tq, tk, lam_init, dec_seq, pages, steps_per_row):
    del pt_ref
    kc_refs, vc_refs = rest[:pages], rest[pages:2 * pages]
    op_ref, os_ref = rest[2 * pages:2 * pages + 2]
    scratch = rest[2 * pages + 2:]
    prompt_scratch, decode_scratch = scratch[:N_PROMPT_SCRATCH], scratch[N_PROMPT_SCRATCH:]
    head, qi = pl.program_id(0), pl.program_id(1)
    step = head * pl.num_programs(1) + qi
    sub = step % steps_per_row
    last = sub == steps_per_row - 1
    decode_begin, decode_pages, decode_end = _decode_stages(
        relb_ref, lam_ref, grow_ref, qs_ref, kn_ref, vn_ref, kc_refs, vc_refs, os_ref, *decode_scratch,
        dec_seq=dec_seq, lam_init=lam_init, pages_per_update=SAMPLE_PAGES_PER_UPDATE)
    decode_begin(step == 0, sub == 0)
    _prompt_tile(relb_ref, lam_ref, gcol_ref, qt_ref, k_ref, vt_ref, op_ref, *prompt_scratch,
                 tq=tq, tk=tk, lam_init=lam_init, head=head, qi=qi,
                 before_drain=lambda: decode_pages(last))
    decode_end(last)


def _attn_call(qt, kb, vt, q_rows, k_new, v_new, cache_k, cache_v, page_table, rel_bias, lam_vecs,
               subln_g, lam_init, dec_seq, tq, tk):
    b, _, _, s = qt.shape
    nb, rows, _ = q_rows.shape
    n_pages = page_table.shape[1]
    n_steps = N_HEADS * (s // tq)
    assert s % tq == 0 and tq % tk == 0 and tk >= FAR_DISTANCE and PAGE_SIZE >= FAR_DISTANCE + dec_seq
    assert (nb * n_pages) % n_steps == 0
    pages = nb * n_pages // n_steps
    assert n_pages % pages == 0 and pages % SAMPLE_PAGES_PER_UPDATE == 0
    steps_per_row = n_pages // pages
    flat = PAGE_SIZE * N_HEADS
    n_q = s // tq

    def row_of(h, qi):
        step = h * n_q + qi
        return step // steps_per_row, (step % steps_per_row) * pages

    const = lambda shape: pl.BlockSpec(shape, lambda h, qi, pt: (0,) * len(shape),
                                       pipeline_mode=pl.Buffered(1))
    per_row = lambda shape: pl.BlockSpec((None,) + shape, lambda h, qi, pt: (row_of(h, qi)[0], 0, 0))
    paged = lambda g: pl.BlockSpec(
        (None, PAGE_SIZE, N_HEADS, QK_DIM),
        lambda h, qi, pt: (pt[row_of(h, qi)[0], row_of(h, qi)[1] + g], 0, 0, 0))
    out_tile = pl.BlockSpec((b, None, tq, V_DIM), lambda h, qi, pt: (0, h, qi, 0))
    q_tile = pl.BlockSpec((b, None, QK_DIM, tq), lambda h, qi, pt: (0, h, 0, qi))
    k_seq = pl.BlockSpec((b, None, s, QK_DIM), lambda h, qi, pt: (0, h, 0, 0), pipeline_mode=pl.Buffered(1))
    v_seq = pl.BlockSpec((b, None, V_EXT, s), lambda h, qi, pt: (0, h, 0, 0), pipeline_mode=pl.Buffered(1))
    stat = pltpu.VMEM((b, 1, 2 * tq), F32)
    prompt_scratch = [pltpu.VMEM((b, QK_DIM, 2 * tq), BF16), stat,
                      pltpu.VMEM((b, V_EXT, 2 * tq), F32),
                      pltpu.VMEM((tq // tk + 2, tk, tq), F32),
                      pltpu.VMEM((b, tk, 2 * tq), F32), pltpu.VMEM((b, tk, 2 * tq), F32),
                      pltpu.VMEM((b, tk, 2 * tq), BF16), pltpu.VMEM((b, tk, 2 * tq), BF16),
                      stat, stat, stat, stat]
    decode_scratch = [pltpu.VMEM((rows, V7X_LANES), F32),
                      pltpu.VMEM((rows, V7X_LANES), F32),
                      pltpu.VMEM((rows, V_DIM), F32),
                      pltpu.VMEM((rows, QK_DIM), BF16),
                      pltpu.VMEM((2, rows, flat), F32),
                      pltpu.VMEM((rows, k_new.shape[1]), F32)]
    assert len(prompt_scratch) == N_PROMPT_SCRATCH and len(decode_scratch) == N_DECODE_SCRATCH
    grid_spec = pltpu.PrefetchScalarGridSpec(
        num_scalar_prefetch=1,
        grid=(N_HEADS, n_q),
        in_specs=[pl.BlockSpec(memory_space=pltpu.SMEM), const((4, HEAD_DIM)), const((V_DIM, 1)),
                  const((1, V_DIM)), q_tile, k_seq, v_seq,
                  per_row((rows, QK_DIM)), per_row(k_new.shape[1:]), per_row(v_new.shape[1:])]
        + [paged(g) for g in range(pages)] * 2,
        out_specs=[out_tile, per_row((rows, V_DIM))],
        scratch_shapes=prompt_scratch + decode_scratch,
    )
    return pl.pallas_call(
        functools.partial(_attn_kernel, tq=tq, tk=tk, lam_init=lam_init, dec_seq=dec_seq, pages=pages,
                          steps_per_row=steps_per_row),
        grid_spec=grid_spec,
        out_shape=[jax.ShapeDtypeStruct((b, N_HEADS, s, V_DIM), BF16),
                   jax.ShapeDtypeStruct((nb, rows, V_DIM), F32)],
        compiler_params=pltpu.CompilerParams(dimension_semantics=("arbitrary",) * 2,
                                             vmem_limit_bytes=ATTN_VMEM_LIMIT_BYTES),
        name="attention",
    )(page_table, rel_bias, lam_vecs, subln_g.reshape(V_DIM, 1), subln_g.reshape(1, V_DIM), qt, kb, vt,
      q_rows, k_new, v_new, *([cache_k] * pages), *([cache_v] * pages))


def _silu_mul(g, u):
    return (g * (1.0 / (1.0 + jnp.exp(-g))) * u).astype(BF16)


def _with_mixer_branch(x_ref, proj_refs):
    if not proj_refs:
        return x_ref[...]
    a_ref, wo_ref, gmix_ref = proj_refs
    a = jnp.concatenate([a_ref[h] for h in range(N_HEADS)], axis=1)
    return x_ref[...] + _rms(jnp.dot(a, wo_ref[...], preferred_element_type=F32), gmix_ref[...])


def _ffn_kernel(*refs, with_proj):
    x_ref, proj_refs = refs[0], (refs[1:4] if with_proj else ())
    gpre_ref, wg_ref, wu_ref, wd_ref, gpost_ref, o_ref = refs[1 + len(proj_refs):]
    x = _with_mixer_branch(x_ref, proj_refs)
    hb = _rms(x, gpre_ref[...]).astype(BF16)
    g = jnp.dot(hb, wg_ref[...], preferred_element_type=F32)
    u = jnp.dot(hb, wu_ref[...], preferred_element_type=F32)
    y = jnp.dot(_silu_mul(g, u), wd_ref[...], preferred_element_type=F32)
    o_ref[...] = x + _rms(y, gpost_ref[...])


def _ffn_stream_kernel(*refs, with_proj):
    x_ref, proj_refs = refs[0], (refs[1:4] if with_proj else ())
    (gpre_ref, wg_ref, wu_ref, wd_ref, gpost_ref,
     o_ref, wgb_ref, wub_ref, wdb_ref, x_acc_ref, hb_ref, y_ref) = refs[1 + len(proj_refs):]
    c = pl.program_id(0)

    @pl.when(c == 0)
    def _first_chunk():
        x = _with_mixer_branch(x_ref, proj_refs)
        x_acc_ref[...] = x
        hb_ref[...] = _rms(x, gpre_ref[...]).astype(BF16)
        y_ref[...] = jnp.zeros(y_ref.shape, F32)

    wg, wu, wd = wg_ref[...].astype(BF16), wu_ref[...].astype(BF16), wd_ref[...].astype(BF16)
    wgb_ref[...], wub_ref[...], wdb_ref[...] = wg, wu, wd
    hb = hb_ref[...]
    g = jnp.dot(hb, wg, preferred_element_type=F32)
    u = jnp.dot(hb, wu, preferred_element_type=F32)
    y_ref[...] += jnp.dot(_silu_mul(g, u), wd, preferred_element_type=F32)

    @pl.when(c == pl.num_programs(0) - 1)
    def _last_chunk():
        o_ref[...] = x_acc_ref[...] + _rms(y_ref[...], gpost_ref[...])


def _proj_specs(proj, tm, tile_of):
    a_bf16, wo_bf16, g_mix = proj
    tiles_per_seq = a_bf16.shape[2] // tm
    heads = pl.BlockSpec((None, N_HEADS, tm, V_DIM),
                         lambda i: (tile_of(i) // tiles_per_seq, 0, tile_of(i) % tiles_per_seq, 0))
    return [heads, _const_spec(wo_bf16.shape), _const_spec((1, D_MODEL))], (a_bf16, wo_bf16, g_mix)


def _ffn_call(x, g_pre, wg_bf16, wu_bf16, wd_bf16, g_post, tm, proj=None):
    m = x.shape[0]
    row = pl.BlockSpec((tm, D_MODEL), lambda i: (i, 0))
    proj_specs, proj_args = _proj_specs(proj, tm, lambda i: i) if proj else ([], ())
    specs = [row] + proj_specs + [_const_spec((1, D_MODEL)), _const_spec(wg_bf16.shape),
                                  _const_spec(wu_bf16.shape), _const_spec(wd_bf16.shape),
                                  _const_spec((1, D_MODEL))]
    return pl.pallas_call(
        functools.partial(_ffn_kernel, with_proj=proj is not None),
        grid=(m // tm,),
        in_specs=specs,
        out_specs=row,
        out_shape=jax.ShapeDtypeStruct((m, D_MODEL), F32),
        compiler_params=_params(1),
        name="swiglu_ffn",
    )(x, *proj_args, g_pre, wg_bf16, wu_bf16, wd_bf16, g_post)


def _ffn_stream_call(x, g_pre, w_gu, w_down, layer, g_post, proj=None):
    m = x.shape[0]
    d_ff = w_down.shape[1]
    tc = FFN_STREAM_CHUNK
    n_chunks = d_ff // tc
    assert d_ff % tc == 0
    whole = pl.BlockSpec((m, D_MODEL), lambda c: (0, 0))
    proj_specs, proj_args = _proj_specs(proj, m, lambda c: 0) if proj else ([], ())
    specs = [whole] + proj_specs + [
        _const_spec((1, D_MODEL)),
        pl.BlockSpec((None, D_MODEL, tc), lambda c: (layer, 0, c)),
        pl.BlockSpec((None, D_MODEL, tc), lambda c: (layer, 0, n_chunks + c)),
        pl.BlockSpec((None, tc, D_MODEL), lambda c: (layer, c, 0)),
        _const_spec((1, D_MODEL))]
    cols = pl.BlockSpec((D_MODEL, tc), lambda c: (0, c))
    return pl.pallas_call(
        functools.partial(_ffn_stream_kernel, with_proj=proj is not None),
        grid=(n_chunks,),
        in_specs=specs,
        out_specs=[whole, cols, cols, pl.BlockSpec((tc, D_MODEL), lambda c: (c, 0))],
        out_shape=[jax.ShapeDtypeStruct((m, D_MODEL), F32),
                   jax.ShapeDtypeStruct((D_MODEL, d_ff), BF16),
                   jax.ShapeDtypeStruct((D_MODEL, d_ff), BF16),
                   jax.ShapeDtypeStruct((d_ff, D_MODEL), BF16)],
        scratch_shapes=[pltpu.VMEM((m, D_MODEL), F32), pltpu.VMEM((m, D_MODEL), BF16),
                        pltpu.VMEM((m, D_MODEL), F32)],
        compiler_params=_params(1),
        name="swiglu_ffn_stream",
    )(x, *proj_args, g_pre, w_gu, w_gu, w_down, g_post)


def _conv_kernel(*refs, tm, seq):
    carry_mode = seq >= tm
    if carry_mode:
        (x_ref, gpre_ref, win_ref, cw_ref, wout_ref, gpost_ref, st_ref,
         o_ref, tail_ref, carry_ref) = refs
    else:
        (x_ref, gpre_ref, win_ref, cw_ref, wout_ref, gpost_ref, inj1_ref, inj2_ref,
         o_ref, u_ref) = refs
    d = D_MODEL
    x = x_ref[...]
    hb = _rms(x, gpre_ref[...]).astype(BF16)
    gc = jnp.dot(hb, win_ref[:, d:2 * d], preferred_element_type=F32)
    xt = jnp.dot(hb, win_ref[:, 2 * d:3 * d], preferred_element_type=F32)
    u = gc * xt
    row = lax.broadcasted_iota(jnp.int32, (tm, d), 0)
    r1 = pltpu.roll(u, 1, axis=0)
    r2 = pltpu.roll(u, 2, axis=0)
    if carry_mode:
        i = pl.program_id(0)
        tiles_per_seq = seq // tm

        @pl.when(i % tiles_per_seq == 0)
        def _seq_start():
            carry_ref[...] = st_ref[...]

        c = carry_ref[...]
        c1 = c[V7X_SUBLANES - 1:V7X_SUBLANES]
        c2 = c[V7X_SUBLANES - 2:V7X_SUBLANES - 1]
        prev1 = jnp.where(row == 0, c1, r1)
        prev2 = jnp.where(row == 0, c2, jnp.where(row == 1, c1, r2))
        tail = u[tm - V7X_SUBLANES:tm]
        carry_ref[...] = tail
        tail_ref[...] = tail
    else:
        t = row % seq
        prev1 = jnp.where(t >= 1, r1, inj1_ref[...])
        prev2 = jnp.where(t >= 2, r2, inj2_ref[...])
        u_ref[...] = u
    cw = cw_ref[...]
    y = cw[0:1] * prev2 + cw[1:2] * prev1 + cw[2:3] * u
    gb = jnp.dot(hb, win_ref[:, 0:d], preferred_element_type=F32)
    z = jnp.dot((gb * y).astype(BF16), wout_ref[...], preferred_element_type=F32)
    o_ref[...] = x + _rms(z, gpost_ref[...])


def _conv_call(x, g_pre, win_bf16, conv_w, wout_bf16, g_post, state, tm, seq):
    m = x.shape[0]
    nb = m // seq
    row = pl.BlockSpec((tm, D_MODEL), lambda i: (i, 0))
    common = [row, _const_spec((1, D_MODEL)), _const_spec(win_bf16.shape),
              _const_spec((CONV_WIDTH, D_MODEL)), _const_spec(wout_bf16.shape),
              _const_spec((1, D_MODEL))]
    kern = functools.partial(_conv_kernel, tm=tm, seq=seq)
    if seq >= tm:
        assert seq % tm == 0
        n_tiles = m // tm
        tiles_per_seq = seq // tm
        pad = jnp.zeros((nb, V7X_SUBLANES - (CONV_WIDTH - 1), D_MODEL), F32)
        st = jnp.concatenate([pad, state], axis=1)
        st_spec = pl.BlockSpec((None, V7X_SUBLANES, D_MODEL), lambda i: (i // tiles_per_seq, 0, 0))
        tail_spec = pl.BlockSpec((None, V7X_SUBLANES, D_MODEL), lambda i: (i, 0, 0))
        out, tails = pl.pallas_call(
            kern, grid=(n_tiles,),
            in_specs=common + [st_spec],
            out_specs=[row, tail_spec],
            out_shape=[jax.ShapeDtypeStruct((m, D_MODEL), F32),
                       jax.ShapeDtypeStruct((n_tiles, V7X_SUBLANES, D_MODEL), F32)],
            scratch_shapes=[pltpu.VMEM((V7X_SUBLANES, D_MODEL), F32)],
            compiler_params=_params(1),
            name="short_conv_seq",
        )(x, g_pre, win_bf16, conv_w, wout_bf16, g_post, st)
        last = tails.reshape(nb, tiles_per_seq, V7X_SUBLANES, D_MODEL)[:, -1]
        return out, last[:, V7X_SUBLANES - (CONV_WIDTH - 1):]
    assert tm % seq == 0 and m == tm and seq >= CONV_WIDTH - 1
    zeros = lambda n: jnp.zeros((nb, n, D_MODEL), F32)
    inj1 = jnp.concatenate([state[:, 1:2], zeros(seq - 1)], axis=1).reshape(m, D_MODEL)
    inj2 = jnp.concatenate([state[:, 0:2], zeros(seq - 2)], axis=1).reshape(m, D_MODEL)
    out, u = pl.pallas_call(
        kern, grid=(1,),
        in_specs=common + [row, row],
        out_specs=[row, row],
        out_shape=[jax.ShapeDtypeStruct((m, D_MODEL), F32)] * 2,
        compiler_params=_params(1),
        name="short_conv_tok",
    )(x, g_pre, win_bf16, conv_w, wout_bf16, g_post, inj1, inj2)
    return out, u.reshape(nb, seq, D_MODEL)[:, seq - (CONV_WIDTH - 1):]


def kernel(x_prompt, x_sample, cache_k, cache_v, state_conv, page_table, rel_bias, w_qkv, w_o, lambda_q1, lambda_k1, lambda_q2, lambda_k2, subln_g, w_conv_in, conv_w, w_conv_out, norm_mix_pre, norm_mix_post, norm_ffn_pre, norm_ffn_post, w_ffn_gu, w_ffn_down):
    b, s, d = x_prompt.shape
    nb, dec_seq, _ = x_sample.shape
    assert d == D_MODEL and page_table.shape[1] * PAGE_SIZE >= FAR_DISTANCE
    mp, ms = b * s, nb * dec_seq
    tm_p = TOKEN_TILE

    wqkv = w_qkv.astype(BF16)
    wo = w_o.astype(BF16)
    win = w_conv_in.astype(BF16)
    wout = w_conv_out.astype(BF16)
    gain =lambda g, i: g[i].reshape(1, D_MODEL)
    lam_vecs = jnp.stack([lambda_q1, lambda_k1, lambda_q2, lambda_k2])
    lam0 = _lambda_init(0)

    xp = x_prompt.reshape(mp, d)
    xs = x_sample.reshape(ms, d)

    g_pre = gain(norm_mix_pre, 0)
    kf_p, vf_p, kb_p, qt_p, vt_p = _qkv_call(xp, g_pre, wqkv, tm_p, s)
    kf_s, vf_s, kb_s, qt_s, _ = _qkv_call(xs, g_pre, wqkv, ms, ms)

    q4 = qt_s.reshape(N_HEADS, QK_DIM, nb, dec_seq).transpose(2, 0, 3, 1)
    q_rows = jnp.stack([q4, q4], axis=2).reshape(nb, N_HEADS * 2 * dec_seq, QK_DIM)
    pad_tok = PAGE_SIZE // N_HEADS - dec_seq
    flat_new = lambda a: jnp.pad(a.reshape(nb, dec_seq, N_HEADS, QK_DIM),
                                 ((0, 0), (0, pad_tok), (0, 0), (0, 0))).reshape(nb, PAGE_SIZE, QK_DIM)
    on_p, on_rows = _attn_call(qt_p, kb_p, vt_p, q_rows, flat_new(kb_s[0].transpose(1, 0, 2)),
                               flat_new(vf_s.astype(BF16)), cache_k, cache_v, page_table, rel_bias,
                               lam_vecs, subln_g, lam0, dec_seq, ATTN_QUERY_TILE, ATTN_KEY_TILE)
    on_s = on_rows.reshape(nb, N_HEADS, 2, dec_seq, V_DIM)[:, :, 0].transpose(1, 0, 2, 3)
    on_s = on_s.reshape(1, N_HEADS, ms, V_DIM).astype(BF16)

    g_post = gain(norm_mix_post, 0)

    def ffn_both(xs_, xp_, i, proj_s=None, proj_p=None):
        xs_, wg, wu, wd = _ffn_stream_call(xs_, gain(norm_ffn_pre, i), w_ffn_gu, w_ffn_down, i,
                                           gain(norm_ffn_post, i), proj_s)
        xp_ = _ffn_call(xp_, gain(norm_ffn_pre, i), wg, wu, wd, gain(norm_ffn_post, i), tm_p, proj_p)
        return xs_, xp_

    xs, xp = ffn_both(xs, xp, 0, (on_s, wo, g_post), (on_p, wo, g_post))

    g_pre, g_post = gain(norm_mix_pre, 1), gain(norm_mix_post, 1)
    zero_state = jnp.zeros((b, CONV_WIDTH - 1, d), F32)
    xp, conv_p = _conv_call(xp, g_pre, win, conv_w, wout, g_post, zero_state, tm_p, s)
    xs, conv_s = _conv_call(xs, g_pre, win, conv_w, wout, g_post, state_conv, ms, dec_seq)
    xs, xp = ffn_both(xs, xp, 1)

    return (xp.reshape(b, s, d), xs.reshape(nb, dec_seq, d),
            kf_p.reshape(b, s, N_HEADS, QK_DIM), vf_p.reshape(b, s, N_HEADS, V_DIM),
            kf_s.reshape(nb, dec_seq, N_HEADS, QK_DIM), vf_s.reshape(nb, dec_seq, N_HEADS, V_DIM),
            conv_p, conv_s)
```
